```python
import jax, jax.numpy as jnp
from jax import lax
import numpy as np

D_MODEL = 1024
BATCH = 8
SEQ = 4096
DEPTH = 4

GRID_W = 64
RMS_EPS = 1e-6
LB_FLOOR = 1e-12
MASK_VALUE = -1e30
A_DK = 128
A_DV = A_DK
A_WIDTH = D_MODEL // 2
A_HEADS = A_WIDTH // A_DK
HGRN_CHUNK = 32
B_DH = 64
B_WIDTH = D_MODEL - A_WIDTH
B_HEADS = B_WIDTH // B_DH
NA_KR = 8
NA_KC = 16
C_CONFIGS = ((128, 1), (512, 4), (2048, 16))
C_GROUPS = len(C_CONFIGS)
C_DH = 64
C_WIDTH = D_MODEL // 2
C_HPG = C_WIDTH // C_DH
ATTN_BLOCK = 128
POOL_WINDOWS = (2, 4, 8, 16)
POOL_CH = D_MODEL - C_WIDTH
POOL_GC = POOL_CH // len(POOL_WINDOWS)
D_FF = ((8 * D_MODEL // 3 + 255) // 256) * 256
FFN_CONV = 3
EVEN_IN = 5 * A_WIDTH + 3 * B_WIDTH
ODD_IN = 3 * C_GROUPS * C_WIDTH + POOL_CH
MIX_W = D_MODEL
N_EVEN = (DEPTH + 1) // 2
N_ODD = DEPTH // 2

kernel_name = 'hybrid_hgrn2_natten_dilated_pool_encoder'

F32 = jnp.float32


def _rmsnorm(x, g):
    xf = x.astype(F32)
    y = xf * lax.rsqrt(jnp.mean(xf * xf, axis=-1, keepdims=True) + RMS_EPS)
    return (y * g.astype(F32)).astype(x.dtype)


def _alibi_slopes(n):
    return jnp.exp2(-8.0 * jnp.arange(1, n + 1, dtype=F32) / n)


def _gla_scan(q, k, v, log_f):
    B, H, T, DK = q.shape
    DV = v.shape[-1]
    n = T // HGRN_CHUNK

    def chunks(a):
        return jnp.moveaxis(a.reshape(B, H, n, HGRN_CHUNK, a.shape[-1]), 2, 0)

    tril = jnp.tril(jnp.ones((HGRN_CHUNK, HGRN_CHUNK), dtype=bool))[:, :, None]

    def step(S, inp):
        qc, kc, vc, lfc = inp
        b = jnp.cumsum(lfc, axis=2)
        b_last = b[:, :, -1:, :]
        diff = b[:, :, :, None, :] - b[:, :, None, :, :]
        decay = jnp.where(tril, jnp.exp(jnp.where(tril, diff, 0.0)), 0.0)
        att = jnp.einsum('bhck,bhsk,bhcsk->bhcs', qc, kc, decay)
        o = jnp.einsum('bhcs,bhsv->bhcv', att, vc) + jnp.einsum('bhck,bhkv->bhcv', qc * jnp.exp(b), S)
        S = jnp.exp(b_last[:, :, 0, :, None]) * S + jnp.einsum('bhsk,bhsv->bhkv', kc * jnp.exp(b_last - b), vc)
        return S, o

    S0 = jnp.zeros((B, H, DK, DV), F32)
    _, o = lax.scan(step, S0, (chunks(q), chunks(k), chunks(v), chunks(log_f)))
    return jnp.moveaxis(o, 0, 2).reshape(B, H, T, DV)


def _hgrn2(z, lb, norm_g):
    B, T, _ = z.shape
    q, f_fwd, f_bwd, i, g = jnp.split(z.astype(F32), 5, axis=-1)

    def heads(a):
        return a.reshape(B, T, A_HEADS, -1).transpose(0, 2, 1, 3)

    q = heads(jax.nn.silu(q))
    v = heads(i)
    lb = lb.astype(F32).reshape(A_HEADS, 1, A_DK)
    log_lb = jnp.log(jnp.maximum(lb, LB_FLOOR))
    log_1m_lb = jnp.log1p(-lb)

    def gates(fz):
        fz = heads(fz)
        log_f = jnp.logaddexp(log_lb, log_1m_lb + jax.nn.log_sigmoid(fz))
        k = (1.0 - lb) * jax.nn.sigmoid(-fz)
        return k, log_f

    k_f, lf_f = gates(f_fwd)
    k_b, lf_b = gates(f_bwd)
    o_fwd = _gla_scan(q, k_f, v, lf_f)
    flip = lambda a: jnp.flip(a, axis=2)
    o_bwd = flip(_gla_scan(flip(q), flip(k_b), flip(v), flip(lf_b)))
    o = o_fwd + o_bwd
    o = o * lax.rsqrt(jnp.mean(o * o, axis=-1, keepdims=True) + RMS_EPS) * norm_g.astype(F32)
    return o.transpose(0, 2, 1, 3).reshape(B, T, A_WIDTH) * jax.nn.silu(g)


def _neighbourhood_attention(q, k, v, rpb):
    B, T, H, DH = q.shape
    rows = T // GRID_W
    kr = min(NA_KR, rows)
    grid = lambda a: a.astype(F32).reshape(B, rows, GRID_W, H, DH)
    qg, kg, vg = grid(q), grid(k), grid(v)
    col = jnp.arange(GRID_W)
    col_idx = jnp.clip(col - NA_KC // 2, 0, GRID_W - NA_KC)[:, None] + jnp.arange(NA_KC)[None, :]
    rpb_col = rpb.astype(F32)[:, :, col_idx - col[:, None] + NA_KC - 1]
    scale = DH ** -0.5

    def row_block(args):
        r, q_row = args
        rs = jnp.clip(r - kr // 2, 0, rows - kr)
        k_win = lax.dynamic_slice_in_dim(kg, rs, kr, axis=1)[:, :, col_idx]
        v_win = lax.dynamic_slice_in_dim(vg, rs, kr, axis=1)[:, :, col_idx]
        bias = rpb_col[:, rs + jnp.arange(kr) - r + NA_KR - 1]
        s = jnp.einsum('bchd,bicjhd->bhcij', q_row, k_win) * scale + jnp.transpose(bias, (0, 2, 1, 3))[None]
        p = jax.nn.softmax(s.reshape(B, H, GRID_W, kr * NA_KC), axis=-1).reshape(s.shape)
        return jnp.einsum('bhcij,bicjhd->bchd', p, v_win)

    o = lax.map(row_block, (jnp.arange(rows), jnp.moveaxis(qg, 1, 0)))
    return jnp.moveaxis(o, 0, 1).reshape(B, T, H * DH)


def _dilated_attention(q, k, v, slopes):
    B, T = q.shape[:2]
    scale = C_DH ** -0.5

    def block(t0):
        pos = t0 + jnp.arange(ATTN_BLOCK)
        q_blk = lax.dynamic_slice_in_dim(q, t0, ATTN_BLOCK, axis=1).astype(F32)
        outs, lses = [], []
        for g, (window, dil) in enumerate(C_CONFIGS):
            per_side = window // 2 // dil
            offs = dil * jnp.arange(-per_side, per_side + 1)
            idx = pos[:, None] + offs[None, :]
            valid = (idx >= 0) & (idx < T)
            idx = jnp.clip(idx, 0, T - 1)
            k_sel = k[:, idx, g].astype(F32)
            v_sel = v[:, idx, g].astype(F32)
            s = jnp.einsum('bqhd,bqjhd->bhqj', q_blk[:, :, g], k_sel) * scale
            s = s - slopes[g][:, None, None] * jnp.abs(offs).astype(F32)
            s = jnp.where(valid, s, MASK_VALUE)
            m = jnp.max(s, axis=-1, keepdims=True)
            p = jnp.exp(s - m)
            den = jnp.sum(p, axis=-1, keepdims=True)
            outs.append(jnp.einsum('bhqj,bqjhd->bhqd', p / den, v_sel))
            lses.append(m + jnp.log(den))
        w = jax.nn.softmax(jnp.stack(lses), axis=0)
        o = jnp.sum(w * jnp.stack(outs), axis=0)
        return jnp.transpose(o, (0, 2, 1, 3))

    o = lax.map(block, jnp.arange(T // ATTN_BLOCK) * ATTN_BLOCK)
    return jnp.moveaxis(o, 0, 1).reshape(B, T, C_HPG * C_DH)


def _multiscale_pool(u, w_groups, scale):
    B, T, _ = u.shape
    uf = u.astype(F32)
    cs = jnp.concatenate([jnp.zeros((B, 1, POOL_CH), F32), jnp.cumsum(uf, axis=1)], axis=1)
    t = jnp.arange(T)
    outs = []
    for g, w in enumerate(POOL_WINDOWS):
        lo = jnp.clip(t - w // 2, 0, T)
        hi = jnp.clip(t + w - w // 2, 0, T)
        sl = slice(g * POOL_GC, (g + 1) * POOL_GC)
        mean = (cs[:, hi, sl] - cs[:, lo, sl]) / (hi - lo).astype(F32)[None, :, None]
        outs.append(jnp.einsum('btc,cd->btd', mean - uf[:, :, sl], w_groups[g].astype(F32)))
    return (jnp.concatenate(outs, axis=-1) * scale.astype(F32)).astype(u.dtype)


def _conv_ffn(h, w_up, conv_w, conv_b, w_down):
    T = h.shape[1]
    gate, val = jnp.split(h @ w_up, 2, axis=-1)
    gp = jnp.pad(gate, ((0, 0), (FFN_CONV // 2, FFN_CONV // 2), (0, 0)))
    gate = conv_b + sum(gp[:, j:j + T] * conv_w[j] for j in range(FFN_CONV))
    return (jax.nn.silu(gate) * val) @ w_down


def setup_inputs(seed: int = 0) -> dict:
    key = jax.random.key(seed)
    ks = jax.random.split(key, 20)
    nrm = lambda k, shape, s=1.0: s * jax.random.normal(k, shape, F32)
    return {
        'x': nrm(ks[0], (BATCH, SEQ, D_MODEL)),
        'c': nrm(ks[1], (BATCH, D_MODEL)),
        'ada_w': nrm(ks[2], (DEPTH, D_MODEL, 6 * D_MODEL), 0.5 * D_MODEL ** -0.5),
        'ada_b': nrm(ks[3], (DEPTH, 6 * D_MODEL), 0.02),
        'norm_mix_g': 1.0 + nrm(ks[4], (DEPTH, D_MODEL), 0.1),
        'norm_ffn_g': 1.0 + nrm(ks[5], (DEPTH, D_MODEL), 0.1),
        'even_w_in': nrm(ks[6], (N_EVEN, D_MODEL, EVEN_IN), D_MODEL ** -0.5),
        'even_w_out': nrm(ks[7], (N_EVEN, MIX_W, D_MODEL), MIX_W ** -0.5),
        'hgrn_lb_logits': nrm(ks[8], (N_EVEN, A_WIDTH)),
        'hgrn_norm_g': 1.0 + nrm(ks[9], (N_EVEN, A_DV), 0.1),
        'na_rpb': nrm(ks[10], (N_EVEN, B_HEADS, 2 * NA_KR - 1, 2 * NA_KC - 1), 0.1),
        'odd_w_in': nrm(ks[11], (N_ODD, D_MODEL, ODD_IN), D_MODEL ** -0.5),
        'odd_w_out': nrm(ks[12], (N_ODD, MIX_W, D_MODEL), MIX_W ** -0.5),
        'pool_w': nrm(ks[13], (N_ODD, len(POOL_WINDOWS), POOL_GC, POOL_GC), POOL_GC ** -0.5),
        'pool_scale': 1.0 + nrm(ks[14], (N_ODD, POOL_CH), 0.1),
        'ffn_w_up': nrm(ks[15], (DEPTH, D_MODEL, 2 * D_FF), D_MODEL ** -0.5),
        'ffn_conv_w': nrm(ks[16], (DEPTH, FFN_CONV, D_FF), FFN_CONV ** -0.5),
        'ffn_conv_b': nrm(ks[17], (DEPTH, D_FF), 0.02),
        'ffn_w_down': nrm(ks[18], (DEPTH, D_FF, D_MODEL), D_FF ** -0.5),
        'final_norm_g': 1.0 + nrm(ks[19], (D_MODEL,), 0.1),
    }


def reference(x, c, ada_w, ada_b, norm_mix_g, norm_ffn_g, even_w_in, even_w_out, hgrn_lb_logits, hgrn_norm_g, na_rpb, odd_w_in, odd_w_out, pool_w, pool_scale, ffn_w_up, ffn_conv_w, ffn_conv_b, ffn_w_down, final_norm_g):
    B, T, _ = x.shape
    lb_soft = jax.nn.softmax(hgrn_lb_logits.astype(F32), axis=0)
    lower_bounds = jnp.cumsum(lb_soft, axis=0) - lb_soft[0]
    slopes = _alibi_slopes(C_GROUPS * C_HPG).reshape(C_GROUPS, C_HPG)
    c_act = jax.nn.silu(c)
    n_c = 3 * C_GROUPS * C_WIDTH
    for l in range(DEPTH):
        mod = c_act @ ada_w[l] + ada_b[l]
        sh1, sc1, g1, sh2, sc2, g2 = jnp.split(mod[:, None, :], 6, axis=-1)
        h = _rmsnorm(x, norm_mix_g[l]) * (1.0 + sc1) + sh1
        if l % 2 == 0:
            e = l // 2
            z = h @ even_w_in[e]
            o_a = _hgrn2(z[..., :5 * A_WIDTH], lower_bounds[e], hgrn_norm_g[e])
            q_b, k_b, v_b = jnp.split(z[..., 5 * A_WIDTH:], 3, axis=-1)
            shape_b = (B, T, B_HEADS, B_DH)
            o_b = _neighbourhood_attention(q_b.reshape(shape_b), k_b.reshape(shape_b), v_b.reshape(shape_b), na_rpb[e])
            mixed = jnp.concatenate([o_a.astype(h.dtype), o_b.astype(h.dtype)], axis=-1) @ even_w_out[e]
        else:
            o_i = l // 2
            z = h @ odd_w_in[o_i]
            q_c, k_c, v_c = jnp.split(z[..., :n_c], 3, axis=-1)
            shape_c = (B, T, C_GROUPS, C_HPG, C_DH)
            o_c = _dilated_attention(q_c.reshape(shape_c), k_c.reshape(shape_c), v_c.reshape(shape_c), slopes)
            o_d = _multiscale_pool(z[..., n_c:], pool_w[o_i], pool_scale[o_i])
            mixed = jnp.concatenate([o_c.astype(h.dtype), o_d], axis=-1) @ odd_w_out[o_i]
        x = x + g1 * mixed
        h = _rmsnorm(x, norm_ffn_g[l]) * (1.0 + sc2) + sh2
        x = x + g2 * _conv_ffn(h, ffn_w_up[l], ffn_conv_w[l], ffn_conv_b[l], ffn_w_down[l])
    return _rmsnorm(x, final_norm_g)
```

```python
import functools

import jax
import jax.numpy as jnp
import numpy as np
from jax import lax
from jax.experimental import pallas as pl
from jax.experimental.pallas import tpu as pltpu

F32 = jnp.float32
BF16 = jnp.bfloat16

GRID_W = 64
RMS_EPS = 1e-6
LB_FLOOR = 1e-12
MASK_VALUE = -1e30
A_DK = 128
HGRN_CHUNK = 32
HGRN_TILE = 128
B_DH = 64
NA_KR = 8
NA_KC = 16
C_CONFIGS = ((128, 1), (512, 4), (2048, 16))
C_DH = 64
C_HPG = 8
ATT_SUB = 128
ATT_HALO = 64
POOL_WINDOWS = (2, 4, 8, 16)
POOL_HALO = 8
FFN_CONV = 3
HALO_ROWS = 8
VMEM_LIMIT = 56 * 1024 * 1024


def _params(sem):
    return pltpu.CompilerParams(dimension_semantics=sem, vmem_limit_bytes=VMEM_LIMIT)


def _silu(v):
    return v * (1.0 / (1.0 + jnp.exp(-v)))


def _norm_mod(x, g, shift, scale):
    ms = jnp.mean(x * x, axis=-1, keepdims=True)
    y = x * lax.rsqrt(ms + RMS_EPS) * g
    return y * (1.0 + scale) + shift


def _ada_kernel(c_ref, w_ref, b_ref, o_ref):
    c = c_ref[...]
    ca = _silu(c).astype(BF16)
    o_ref[0] = jnp.dot(ca, w_ref[0].astype(BF16), preferred_element_type=F32) + b_ref[0]


def _ada_mod(c, ada_w, ada_b):
    depth, d, n = ada_w.shape
    bsz = c.shape[0]
    tn = 1024
    return pl.pallas_call(
        _ada_kernel,
        grid=(depth, n // tn),
        in_specs=[
            pl.BlockSpec((bsz, d), lambda l, j: (0, 0)),
            pl.BlockSpec((1, d, tn), lambda l, j: (l, 0, j)),
            pl.BlockSpec((1, 1, tn), lambda l, j: (l, 0, j)),
        ],
        out_specs=pl.BlockSpec((1, bsz, tn), lambda l, j: (l, 0, j)),
        out_shape=jax.ShapeDtypeStruct((depth, bsz, n), F32),
        compiler_params=_params(("arbitrary", "arbitrary")),
        name="ada_mod",
    )(c, ada_w, ada_b.reshape(depth, 1, n))


def _nmm_kernel(x_ref, g_ref, mod_ref, w_ref, o_ref, h_ref, *, shift_row):
    @pl.when(pl.program_id(2) == 0)
    def _():
        m = mod_ref[0]
        h = _norm_mod(x_ref[0], g_ref[...], m[shift_row:shift_row + 1], m[shift_row + 1:shift_row + 2])
        h_ref[...] = h.astype(BF16)

    o_ref[0] = jnp.dot(h_ref[...], w_ref[...], preferred_element_type=F32).astype(o_ref.dtype)


def _norm_mod_matmul(x, g, mod, w, *, shift_row, out_dtype, tm=1024, tn=512):
    bsz, t, d = x.shape
    n = w.shape[1]
    tn = min(tn, n)
    return pl.pallas_call(
        functools.partial(_nmm_kernel, shift_row=shift_row),
        grid=(bsz, t // tm, n // tn),
        in_specs=[
            pl.BlockSpec((1, tm, d), lambda b, i, j: (b, i, 0)),
            pl.BlockSpec((1, d), lambda b, i, j: (0, 0)),
            pl.BlockSpec((1, 6, d), lambda b, i, j: (b, 0, 0)),
            pl.BlockSpec((d, tn), lambda b, i, j: (0, j)),
        ],
        out_specs=pl.BlockSpec((1, tm, tn), lambda b, i, j: (b, i, j)),
        out_shape=jax.ShapeDtypeStruct((bsz, t, n), out_dtype),
        scratch_shapes=[pltpu.VMEM((tm, d), BF16)],
        compiler_params=_params(("arbitrary", "arbitrary", "arbitrary")),
        name="norm_mod_matmul",
    )(x, g.reshape(1, d), mod, w)


def _hgrn_kernel(q_ref, ff_ref, fb_ref, v_ref, g_ref, lb_ref, ng_ref, o_ref,
                 bf_scr, kf_scr, bb_scr, kb_scr, acc_scr):
    t = q_ref.shape[1]
    c = HGRN_CHUNK
    tl = HGRN_TILE
    n_tiles = t // tl
    n_chunks = t // c

    lb = lb_ref[...]
    lb_floor = jnp.maximum(lb, LB_FLOOR)
    one_m_lb = 1.0 - lb

    row = lax.broadcasted_iota(jnp.int32, (tl, tl), 0)
    col = lax.broadcasted_iota(jnp.int32, (tl, tl), 1)
    same = (row // c) == (col // c)
    tri_f = jnp.where(same & (col <= row), 1.0, 0.0).astype(F32)
    tri_b = jnp.where(same & (col >= row), 1.0, 0.0).astype(F32)
    pos = lax.broadcasted_iota(jnp.int32, (tl, 1), 0) % c

    def gates(fz):
        e = jnp.exp(-jnp.abs(fz))
        r = 1.0 / (1.0 + e)
        sig_pos = jnp.where(fz >= 0, r, e * r)
        sig_neg = jnp.where(fz >= 0, e * r, r)
        return jnp.log(lb_floor + one_m_lb * sig_pos), one_m_lb * sig_neg

    def tile_body(i, carry):
        t0 = pl.multiple_of(i * tl, tl)
        q = _silu(q_ref[0, pl.ds(t0, tl), :])
        v = v_ref[0, pl.ds(t0, tl), :]
        acc = jnp.zeros((tl, A_DK), F32)
        for f_ref, tri, b_scr, k_scr, fwd in ((ff_ref, tri_f, bf_scr, kf_scr, True),
                                              (fb_ref, tri_b, bb_scr, kb_scr, False)):
            lf, kk = gates(f_ref[0, pl.ds(t0, tl), :])
            b = jnp.dot(tri, lf, preferred_element_type=F32, precision=lax.Precision.HIGHEST)
            b_scr[pl.ds(t0, tl), :] = b
            k_scr[pl.ds(t0, tl), :] = kk
            for d in range(c):
                shift = d if fwd else (tl - d) % tl
                if d == 0:
                    kk_d, b_d, v_d = kk, b, v
                else:
                    kk_d = pltpu.roll(kk, shift, 0)
                    b_d = pltpu.roll(b, shift, 0)
                    v_d = pltpu.roll(v, shift, 0)
                ok = (pos >= d) if fwd else (pos + d < c)
                p = jnp.where(ok, q * kk_d * jnp.exp(b - b_d), 0.0)
                acc = acc + jnp.sum(p, axis=-1, keepdims=True) * v_d
        acc_scr[pl.ds(t0, tl), :] = acc
        return carry

    lax.fori_loop(0, n_tiles, tile_body, 0)

    def chunk_step(c0, st, b_scr, k_scr, edge):
        b = b_scr[pl.ds(c0, c), :]
        kk = k_scr[pl.ds(c0, c), :]
        q = _silu(q_ref[0, pl.ds(c0, c), :])
        v = v_ref[0, pl.ds(c0, c), :]
        b_edge = b[edge:edge + 1]
        qb = (q * jnp.exp(b)).astype(BF16)
        o_inter = lax.dot_general(qb, st.astype(BF16), (((1,), (1,)), ((), ())),
                                  preferred_element_type=F32)
        acc_scr[pl.ds(c0, c), :] += o_inter
        kd = kk * jnp.exp(b_edge - b)
        kv = jnp.dot(v.T.astype(BF16), kd.astype(BF16), preferred_element_type=F32)
        return st * jnp.exp(b_edge) + kv

    def scan_body(n, carry):
        st_f, st_b = carry
        cf = pl.multiple_of(n * c, c)
        cb = pl.multiple_of((n_chunks - 1 - n) * c, c)
        st_f = chunk_step(cf, st_f, bf_scr, kf_scr, c - 1)
        st_b = chunk_step(cb, st_b, bb_scr, kb_scr, 0)
        return st_f, st_b

    zero = jnp.zeros((A_DK, A_DK), F32)
    lax.fori_loop(0, n_chunks, scan_body, (zero, zero))

    ng = ng_ref[...]

    def out_body(i, carry):
        t0 = pl.multiple_of(i * tl, tl)
        o = acc_scr[pl.ds(t0, tl), :]
        o = o * lax.rsqrt(jnp.mean(o * o, axis=-1, keepdims=True) + RMS_EPS) * ng
        o_ref[0, pl.ds(t0, tl), :] = (o * _silu(g_ref[0, pl.ds(t0, tl), :])).astype(o_ref.dtype)
        return carry

    lax.fori_loop(0, n_tiles, out_body, 0)


def _hgrn2(z, lb, norm_g):
    bsz, t, w5 = z.shape
    heads = w5 // 5 // A_DK

    def col(k):
        return pl.BlockSpec((1, t, A_DK), lambda b, h, k=k: (b, 0, h + heads * k))

    return pl.pallas_call(
        _hgrn_kernel,
        grid=(bsz, heads),
        in_specs=[col(0), col(1), col(2), col(3), col(4),
                  pl.BlockSpec((1, A_DK), lambda b, h: (0, h)),
                  pl.BlockSpec((1, A_DK), lambda b, h: (0, 0))],
        out_specs=pl.BlockSpec((1, t, A_DK), lambda b, h: (b, 0, h)),
        out_shape=jax.ShapeDtypeStruct((bsz, t, heads * A_DK), BF16),
        scratch_shapes=[pltpu.VMEM((t, A_DK), F32)] * 5,
        compiler_params=_params(("arbitrary", "arbitrary")),
        name="hgrn2",
    )(z, z, z, z, z, lb.reshape(1, heads * A_DK), norm_g.reshape(1, A_DK))


def _lane_half_masks(dtype):
    lane = lax.broadcasted_iota(jnp.int32, (1, 2 * C_DH), 1)
    lo = lane < C_DH
    return lo, jnp.where(lo, 1.0, 0.0).astype(dtype), jnp.where(lo, 0.0, 1.0).astype(dtype)


def _pair_attention(q_pair, k_pair, v_pair, bias_even, bias_odd, extra_ok):
    lo, m_lo, m_hi = _lane_half_masks(q_pair.dtype)
    outs, lses = [], []
    for mask, bias in ((m_lo, bias_even), (m_hi, bias_odd)):
        s = lax.dot_general(q_pair * mask, k_pair, (((1,), (1,)), ((), ())), preferred_element_type=F32)
        s = s + bias
        if extra_ok is not None:
            s = jnp.where(extra_ok, s, MASK_VALUE)
        m = jnp.max(s, axis=-1, keepdims=True)
        p = jnp.exp(s - m)
        den = jnp.sum(p, axis=-1, keepdims=True)
        o = jnp.dot(p.astype(BF16), v_pair, preferred_element_type=F32)
        outs.append(o / den)
        lses.append(m + jnp.log(den))
    out = jnp.where(lo, outs[0], outs[1])
    lse = jnp.where(lo, lses[0], lses[1])
    return out, lse


def _na_kernel(q_ref, k_ref, v_ref, bias_ref, o_ref, *, rows, kr):
    r = pl.program_id(1)
    rs = jnp.clip(r - kr // 2, 0, rows - kr)
    k0 = pl.multiple_of(rs * GRID_W, GRID_W)
    nk = kr * GRID_W
    scale = B_DH ** -0.5
    q = q_ref[0] * scale
    for pr in range(q.shape[-1] // (2 * B_DH)):
        sl = slice(pr * 2 * B_DH, (pr + 1) * 2 * B_DH)
        k_pair = k_ref[0, pl.ds(k0, nk), sl]
        v_pair = v_ref[0, pl.ds(k0, nk), sl]
        out, _ = _pair_attention(q[:, sl], k_pair, v_pair, bias_ref[0, 2 * pr], bias_ref[0, 2 * pr + 1], None)
        o_ref[0, :, sl] = out.astype(o_ref.dtype)


def _na_bias_table(rpb, rows):
    heads = rpb.shape[0]
    kr = min(NA_KR, rows)
    col = np.arange(GRID_W)
    cs = np.clip(col - NA_KC // 2, 0, GRID_W - NA_KC)
    kc = np.arange(GRID_W)
    rel = kc[None, :] - col[:, None] + NA_KC - 1
    ok = (kc[None, :] >= cs[:, None]) & (kc[None, :] < cs[:, None] + NA_KC)
    rel_c = np.clip(rel, 0, 2 * NA_KC - 2)
    full = jnp.where(ok[None, None], rpb.astype(F32)[:, :, rel_c], MASK_VALUE)
    variants = []
    for var in range(kr):
        di0 = NA_KR - 1 - var
        blk = full[:, di0:di0 + kr]
        variants.append(jnp.transpose(blk, (0, 2, 1, 3)).reshape(heads, GRID_W, kr * GRID_W))
    return jnp.stack(variants)


def _neighbourhood_attention(zb, rpb):
    bsz, t, w3 = zb.shape
    width = w3 // 3
    heads = width // B_DH
    rows = t // GRID_W
    kr = min(NA_KR, rows)
    table = _na_bias_table(rpb, rows)

    def var_of(r):
        return r - jnp.clip(r - kr // 2, 0, rows - kr)

    return pl.pallas_call(
        functools.partial(_na_kernel, rows=rows, kr=kr),
        grid=(bsz, rows),
        in_specs=[
            pl.BlockSpec((1, GRID_W, width), lambda b, r: (b, r, 0)),
            pl.BlockSpec((1, t, width), lambda b, r: (b, 0, 1)),
            pl.BlockSpec((1, t, width), lambda b, r: (b, 0, 2)),
            pl.BlockSpec((1, heads, GRID_W, kr * GRID_W), lambda b, r: (var_of(r), 0, 0, 0)),
        ],
        out_specs=pl.BlockSpec((1, GRID_W, width), lambda b, r: (b, r, 0)),
        out_shape=jax.ShapeDtypeStruct((bsz, t, width), BF16),
        compiler_params=_params(("arbitrary", "arbitrary")),
        name="neighbourhood_attention",
    )(zb, zb, zb, table)


def _dil_kernel(q_ref, kp_ref, k_ref, kn_ref, vp_ref, v_ref, vn_ref, bias_ref, o_ref, lse_ref,
                k_scr, v_scr, *, seq_len):
    i = pl.program_id(2)
    tq = q_ref.shape[1]
    t0 = i * tq
    k_scr[0:ATT_HALO] = kp_ref[0]
    k_scr[ATT_HALO:ATT_HALO + tq] = k_ref[0]
    k_scr[ATT_HALO + tq:] = kn_ref[0]
    v_scr[0:ATT_HALO] = vp_ref[0]
    v_scr[ATT_HALO:ATT_HALO + tq] = v_ref[0]
    v_scr[ATT_HALO + tq:] = vn_ref[0]
    scale = C_DH ** -0.5
    nk = ATT_SUB + 2 * ATT_HALO
    kcol = lax.broadcasted_iota(jnp.int32, (1, nk), 1)
    for sb in range(tq // ATT_SUB):
        r0 = sb * ATT_SUB
        kpos = t0 + r0 - ATT_HALO + kcol
        ok = (kpos >= 0) & (kpos < seq_len)
        q = q_ref[0, r0:r0 + ATT_SUB, :] * scale
        for pr in range(q.shape[-1] // (2 * C_DH)):
            sl = slice(pr * 2 * C_DH, (pr + 1) * 2 * C_DH)
            out, lse = _pair_attention(q[:, sl], k_scr[r0:r0 + nk, sl], v_scr[r0:r0 + nk, sl],
                                       bias_ref[2 * pr], bias_ref[2 * pr + 1], ok)
            o_ref[0, r0:r0 + ATT_SUB, sl] = out
            lse_ref[0, r0:r0 + ATT_SUB, sl] = lse


def _alibi_band(slopes_g, dil):
    rel = np.arange(ATT_SUB + 2 * ATT_HALO)[None, :] - ATT_HALO - np.arange(ATT_SUB)[:, None]
    band = np.abs(rel) <= ATT_HALO
    dist = jnp.asarray(np.abs(rel) * dil, F32)
    return jnp.where(band[None], -slopes_g[:, None, None] * dist[None], MASK_VALUE)


def _dilated_group(zc, g, dil, slopes_g, n_groups, tq=256):
    bsz, t, f = zc.shape
    width = f // (3 * n_groups)
    seq = t // dil
    tq = min(tq, seq)
    cb = f // width
    zv = zc.reshape(bsz, seq, dil * f)
    hb = tq // ATT_HALO
    last = seq // ATT_HALO - 1
    bias = _alibi_band(slopes_g, dil)

    def main(off):
        return pl.BlockSpec((1, tq, width), lambda b, r, i: (b, i, r * cb + off))

    def prev(off):
        return pl.BlockSpec((1, ATT_HALO, width), lambda b, r, i: (b, jnp.maximum(i * hb - 1, 0), r * cb + off))

    def nxt(off):
        return pl.BlockSpec((1, ATT_HALO, width),
                            lambda b, r, i: (b, jnp.minimum((i + 1) * hb, last), r * cb + off))

    qo, ko, vo = g, n_groups + g, 2 * n_groups + g
    out_spec = pl.BlockSpec((1, tq, width), lambda b, r, i: (b, i, r))
    o, lse = pl.pallas_call(
        functools.partial(_dil_kernel, seq_len=seq),
        grid=(bsz, dil, seq // tq),
        in_specs=[main(qo), prev(ko), main(ko), nxt(ko), prev(vo), main(vo), nxt(vo),
                  pl.BlockSpec(bias.shape, lambda b, r, i: (0, 0, 0))],
        out_specs=[out_spec, out_spec],
        out_shape=[jax.ShapeDtypeStruct((bsz, seq, dil * width), F32)] * 2,
        scratch_shapes=[pltpu.VMEM((tq + 2 * ATT_HALO, width), BF16)] * 2,
        compiler_params=_params(("arbitrary", "arbitrary", "arbitrary")),
        name=f"dilated_attention_d{dil}",
    )(zv, zv, zv, zv, zv, zv, zv, bias)
    return o.reshape(bsz, t, width), lse.reshape(bsz, t, width)


def _pool_kernel(up_ref, u_ref, un_ref, w_ref, sc_ref, o_ref, scr, *, seq_len):
    i = pl.program_id(1)
    tm = u_ref.shape[1]
    gc = w_ref.shape[1]
    scr[0:HALO_ROWS] = jnp.where(i > 0, up_ref[0], 0.0)
    scr[HALO_ROWS:HALO_ROWS + tm] = u_ref[0]
    scr[HALO_ROWS + tm:] = jnp.where(i < pl.num_programs(1) - 1, un_ref[0], 0.0)
    tpos = i * tm + lax.broadcasted_iota(jnp.int32, (tm, 1), 0)
    for g, w in enumerate(POOL_WINDOWS):
        sl = slice(g * gc, (g + 1) * gc)
        tot = jnp.zeros((tm, gc), F32)
        for off in range(-(w // 2), w - w // 2):
            tot = tot + scr[HALO_ROWS + off:HALO_ROWS + off + tm, sl]
        lo = jnp.clip(tpos - w // 2, 0, seq_len)
        hi = jnp.clip(tpos + w - w // 2, 0, seq_len)
        mean = tot / (hi - lo).astype(F32)
        diff = mean - scr[HALO_ROWS:HALO_ROWS + tm, sl]
        y = jnp.dot(diff.astype(BF16), w_ref[g], preferred_element_type=F32)
        o_ref[0, :, sl] = (y * sc_ref[:, sl]).astype(o_ref.dtype)


def _multiscale_pool(u, w_groups, scale, tm=512):
    bsz, t, ch = u.shape
    hb = tm // HALO_ROWS
    last = t // HALO_ROWS - 1
    return pl.pallas_call(
        functools.partial(_pool_kernel, seq_len=t),
        grid=(bsz, t // tm),
        in_specs=[
            pl.BlockSpec((1, HALO_ROWS, ch), lambda b, i: (b, jnp.maximum(i * hb - 1, 0), 0)),
            pl.BlockSpec((1, tm, ch), lambda b, i: (b, i, 0)),
            pl.BlockSpec((1, HALO_ROWS, ch), lambda b, i: (b, jnp.minimum((i + 1) * hb, last), 0)),
            pl.BlockSpec(w_groups.shape, lambda b, i: (0, 0, 0)),
            pl.BlockSpec((1, ch), lambda b, i: (0, 0)),
        ],
        out_specs=pl.BlockSpec((1, tm, ch), lambda b, i: (b, i, 0)),
        out_shape=jax.ShapeDtypeStruct((bsz, t, ch), BF16),
        scratch_shapes=[pltpu.VMEM((tm + 2 * HALO_ROWS, ch), F32)],
        compiler_params=_params(("arbitrary", "arbitrary")),
        name="multiscale_pool",
    )(u, u, u, w_groups.astype(BF16), scale.reshape(1, ch))


def _out_even_kernel(oa_ref, ob_ref, wa_ref, wb_ref, x_ref, mod_ref, o_ref, *, gate_row):
    mixed = jnp.dot(oa_ref[0], wa_ref[...], preferred_element_type=F32)
    mixed = mixed + jnp.dot(ob_ref[0], wb_ref[...], preferred_element_type=F32)
    gate = mod_ref[0][gate_row:gate_row + 1]
    o_ref[0] = x_ref[0] + gate * mixed


def _out_odd_kernel(o0_ref, o1_ref, o2_ref, l0_ref, l1_ref, l2_ref, od_ref, wa_ref, wb_ref, x_ref, mod_ref,
                    o_ref, *, gate_row):
    l0, l1, l2 = l0_ref[0], l1_ref[0], l2_ref[0]
    m = jnp.maximum(jnp.maximum(l0, l1), l2)
    e0, e1, e2 = jnp.exp(l0 - m), jnp.exp(l1 - m), jnp.exp(l2 - m)
    oc = (e0 * o0_ref[0] + e1 * o1_ref[0] + e2 * o2_ref[0]) / (e0 + e1 + e2)
    mixed = jnp.dot(oc.astype(BF16), wa_ref[...], preferred_element_type=F32)
    mixed = mixed + jnp.dot(od_ref[0], wb_ref[...], preferred_element_type=F32)
    gate = mod_ref[0][gate_row:gate_row + 1]
    o_ref[0] = x_ref[0] + gate * mixed


def _out_proj(parts, w_out, x, mod, *, gate_row, odd, tm=512):
    bsz, t, d = x.shape
    half = w_out.shape[0] // 2
    wa, wb = w_out[:half].astype(BF16), w_out[half:].astype(BF16)
    tok = lambda width: pl.BlockSpec((1, tm, width), lambda b, i: (b, i, 0))
    wspec = pl.BlockSpec((half, d), lambda b, i: (0, 0))
    kern = _out_odd_kernel if odd else _out_even_kernel
    return pl.pallas_call(
        functools.partial(kern, gate_row=gate_row),
        grid=(bsz, t // tm),
        in_specs=[tok(half)] * len(parts) + [wspec, wspec, tok(d), pl.BlockSpec((1, 6, d), lambda b, i: (b, 0, 0))],
        out_specs=tok(d),
        out_shape=jax.ShapeDtypeStruct((bsz, t, d), F32),
        compiler_params=_params(("arbitrary", "arbitrary")),
        name="out_proj_odd" if odd else "out_proj_even",
    )(*parts, wa, wb, x, mod)


def _ffn_up_kernel(xp_ref, x_ref, xn_ref, g_ref, mod_ref, wg_ref, wv_ref, cw_ref, cb_ref, o_ref,
                   h_scr, gate_scr, *, shift_row):
    i = pl.program_id(1)
    tm = x_ref.shape[1]

    @pl.when(pl.program_id(2) == 0)
    def _():
        m = mod_ref[0]
        shift, scale = m[shift_row:shift_row + 1], m[shift_row + 1:shift_row + 2]
        g = g_ref[...]
        h_scr[0:HALO_ROWS] = _norm_mod(xp_ref[0], g, shift, scale).astype(BF16)
        h_scr[HALO_ROWS:HALO_ROWS + tm] = _norm_mod(x_ref[0], g, shift, scale).astype(BF16)
        h_scr[HALO_ROWS + tm:] = _norm_mod(xn_ref[0], g, shift, scale).astype(BF16)

    gate = jnp.dot(h_scr[...], wg_ref[...], preferred_element_type=F32)
    val = jnp.dot(h_scr[HALO_ROWS:HALO_ROWS + tm], wv_ref[...], preferred_element_type=F32)
    row = lax.broadcasted_iota(jnp.int32, (tm + 2 * HALO_ROWS, 1), 0)
    inside = ((row >= HALO_ROWS) | (i > 0)) & ((row < HALO_ROWS + tm) | (i < pl.num_programs(1) - 1))
    gate_scr[...] = jnp.where(inside, gate, 0.0)
    cw = cw_ref[...]
    conv = cb_ref[...]
    for j in range(FFN_CONV):
        a = HALO_ROWS + j - FFN_CONV // 2
        conv = conv + gate_scr[a:a + tm, :] * cw[j:j + 1]
    o_ref[0] = (_silu(conv) * val).astype(o_ref.dtype)


def _ffn_up(x, g, mod, w_up, conv_w, conv_b, *, shift_row, tm=512, tn=1408):
    bsz, t, d = x.shape
    dff = w_up.shape[1] // 2
    nj = dff // tn
    hb = tm // HALO_ROWS
    last = t // HALO_ROWS - 1
    wu = w_up.astype(BF16)
    return pl.pallas_call(
        functools.partial(_ffn_up_kernel, shift_row=shift_row),
        grid=(bsz, t // tm, nj),
        in_specs=[
            pl.BlockSpec((1, HALO_ROWS, d), lambda b, i, j: (b, jnp.maximum(i * hb - 1, 0), 0)),
            pl.BlockSpec((1, tm, d), lambda b, i, j: (b, i, 0)),
            pl.BlockSpec((1, HALO_ROWS, d), lambda b, i, j: (b, jnp.minimum((i + 1) * hb, last), 0)),
            pl.BlockSpec((1, d), lambda b, i, j: (0, 0)),
            pl.BlockSpec((1, 6, d), lambda b, i, j: (b, 0, 0)),
            pl.BlockSpec((d, tn), lambda b, i, j: (0, j)),
            pl.BlockSpec((d, tn), lambda b, i, j: (0, j + nj)),
            pl.BlockSpec((FFN_CONV, tn), lambda b, i, j: (0, j)),
            pl.BlockSpec((1, tn), lambda b, i, j: (0, j)),
        ],
        out_specs=pl.BlockSpec((1, tm, tn), lambda b, i, j: (b, i, j)),
        out_shape=jax.ShapeDtypeStruct((bsz, t, dff), BF16),
        scratch_shapes=[pltpu.VMEM((tm + 2 * HALO_ROWS, d), BF16),
                        pltpu.VMEM((tm + 2 * HALO_ROWS, tn), F32)],
        compiler_params=_params(("arbitrary", "arbitrary", "arbitrary")),
        name="ffn_up_conv",
    )(x, x, x, g.reshape(1, d), mod, wu, wu, conv_w, conv_b.reshape(1, dff))


def _ffn_down_kernel(a_ref, w_ref, x_ref, mod_ref, o_ref, *, gate_row):
    y = jnp.dot(a_ref[0], w_ref[...], preferred_element_type=F32)
    o_ref[0] = x_ref[0] + mod_ref[0][gate_row:gate_row + 1] * y


def _ffn_down(a, w_down, x, mod, *, gate_row, tm=512):
    bsz, t, d = x.shape
    dff = a.shape[-1]
    return pl.pallas_call(
        functools.partial(_ffn_down_kernel, gate_row=gate_row),
        grid=(bsz, t // tm),
        in_specs=[
            pl.BlockSpec((1, tm, dff), lambda b, i: (b, i, 0)),
            pl.BlockSpec((dff, d), lambda b, i: (0, 0)),
            pl.BlockSpec((1, tm, d), lambda b, i: (b, i, 0)),
            pl.BlockSpec((1, 6, d), lambda b, i: (b, 0, 0)),
        ],
        out_specs=pl.BlockSpec((1, tm, d), lambda b, i: (b, i, 0)),
        out_shape=jax.ShapeDtypeStruct((bsz, t, d), F32),
        compiler_params=_params(("arbitrary", "arbitrary")),
        name="ffn_down",
    )(a, w_down.astype(BF16), x, mod)


def _final_norm_kernel(x_ref, g_ref, o_ref):
    x = x_ref[0]
    o_ref[0] = x * lax.rsqrt(jnp.mean(x * x, axis=-1, keepdims=True) + RMS_EPS) * g_ref[...]


def _final_norm(x, g, tm=1024):
    bsz, t, d = x.shape
    return pl.pallas_call(
        _final_norm_kernel,
        grid=(bsz, t // tm),
        in_specs=[pl.BlockSpec((1, tm, d), lambda b, i: (b, i, 0)), pl.BlockSpec((1, d), lambda b, i: (0, 0))],
        out_specs=pl.BlockSpec((1, tm, d), lambda b, i: (b, i, 0)),
        out_shape=jax.ShapeDtypeStruct((bsz, t, d), F32),
        compiler_params=_params(("arbitrary", "arbitrary")),
        name="final_norm",
    )(x, g.reshape(1, d))


def kernel(x, c, ada_w, ada_b, norm_mix_g, norm_ffn_g, even_w_in, even_w_out, hgrn_lb_logits, hgrn_norm_g, na_rpb, odd_w_in, odd_w_out, pool_w, pool_scale, ffn_w_up, ffn_conv_w, ffn_conv_b, ffn_w_down, final_norm_g):
    bsz, t, d = x.shape
    depth = ada_w.shape[0]
    a_width = hgrn_lb_logits.shape[1]
    n_groups = len(C_CONFIGS)
    c_width = C_HPG * C_DH
    n_c = 3 * n_groups * c_width

    lb_soft = jax.nn.softmax(hgrn_lb_logits.astype(F32), axis=0)
    lower_bounds = jnp.cumsum(lb_soft, axis=0) - lb_soft[0]
    slopes = jnp.exp2(-8.0 * jnp.arange(1, n_groups * C_HPG + 1, dtype=F32) / (n_groups * C_HPG))
    slopes = slopes.reshape(n_groups, C_HPG)

    mod_all = _ada_mod(c, ada_w, ada_b).reshape(depth, bsz, 6, d)

    for l in range(depth):
        mod = mod_all[l]
        if l % 2 == 0:
            e = l // 2
            w_in = even_w_in[e].astype(BF16)
            z_a = _norm_mod_matmul(x, norm_mix_g[l], mod, w_in[:, :5 * a_width], shift_row=0, out_dtype=F32)
            z_b = _norm_mod_matmul(x, norm_mix_g[l], mod, w_in[:, 5 * a_width:], shift_row=0, out_dtype=BF16)
            o_a = _hgrn2(z_a, lower_bounds[e], hgrn_norm_g[e])
            o_b = _neighbourhood_attention(z_b, na_rpb[e])
            x = _out_proj([o_a, o_b], even_w_out[e], x, mod, gate_row=2, odd=False)
        else:
            o_i = l // 2
            w_in = odd_w_in[o_i].astype(BF16)
            z_c = _norm_mod_matmul(x, norm_mix_g[l], mod, w_in[:, :n_c], shift_row=0, out_dtype=BF16)
            z_d = _norm_mod_matmul(x, norm_mix_g[l], mod, w_in[:, n_c:], shift_row=0, out_dtype=F32)
            outs, lses = [], []
            for g, (_, dil) in enumerate(C_CONFIGS):
                o_g, lse_g = _dilated_group(z_c, g, dil, slopes[g], n_groups)
                outs.append(o_g)
                lses.append(lse_g)
            o_d = _multiscale_pool(z_d, pool_w[o_i], pool_scale[o_i])
            x = _out_proj(outs + lses + [o_d], odd_w_out[o_i], x, mod, gate_row=2, odd=True)
        a = _ffn_up(x, norm_ffn_g[l], mod, ffn_w_up[l], ffn_conv_w[l], ffn_conv_b[l], shift_row=3)
        x = _ffn_down(a, ffn_w_down[l], x, mod, gate_row=5)
    return _final_norm(x, final_norm_g)
```

```python
import functools

import jax
import jax.numpy as jnp
import numpy as np
from jax import lax
from jax.experimental import pallas as pl
from jax.experimental.pallas import tpu as pltpu

F32 = jnp.float32
BF16 = jnp.bfloat16

GRID_W = 64
RMS_EPS = 1e-6
LB_FLOOR = 1e-12
MASK_VALUE = -1e30
A_DK = 128
HGRN_CHUNK = 32
HGRN_TILE = 128
ATT_DH = 64
ATT_SLAB = 256
ATT_HPS = ATT_SLAB // ATT_DH
NA_KR = 8
NA_KC = 16
C_CONFIGS = ((128, 1), (512, 4), (2048, 16))
C_HPG = 8
DIL_TILE = {1: 1024, 4: 1024, 16: 2048}
ATT_SUB = 128
ATT_HALO = 64
POOL_WINDOWS = (2, 4, 8, 16)
POOL_HALO = 8
FFN_CONV = 3
HALO_ROWS = 8
LANES = 128
VMEM_LIMIT = 56 * 1024 * 1024


def _params(sem):
    return pltpu.CompilerParams(dimension_semantics=sem, vmem_limit_bytes=VMEM_LIMIT)


def _silu(v):
    return v * (1.0 / (1.0 + jnp.exp(-v)))


def _norm_mod(x, g, shift, scale):
    ms = jnp.mean(x * x, axis=-1, keepdims=True)
    y = x * lax.rsqrt(ms + RMS_EPS) * g
    return y * (1.0 + scale) + shift


def _ada_kernel(c_ref, w_ref, b_ref, o_ref):
    c = c_ref[...]
    ca = _silu(c).astype(BF16)
    o_ref[0] = jnp.dot(ca, w_ref[0].astype(BF16), preferred_element_type=F32) + b_ref[0]


def _ada_mod(c, ada_w, ada_b):
    depth, d, n = ada_w.shape
    bsz = c.shape[0]
    tn = 1024
    return pl.pallas_call(
        _ada_kernel,
        grid=(depth, n // tn),
        in_specs=[
            pl.BlockSpec((bsz, d), lambda l, j: (0, 0)),
            pl.BlockSpec((1, d, tn), lambda l, j: (l, 0, j)),
            pl.BlockSpec((1, 1, tn), lambda l, j: (l, 0, j)),
        ],
        out_specs=pl.BlockSpec((1, bsz, tn), lambda l, j: (l, 0, j)),
        out_shape=jax.ShapeDtypeStruct((depth, bsz, n), F32),
        compiler_params=_params(("arbitrary", "arbitrary")),
        name="ada_mod",
    )(c, ada_w, ada_b.reshape(depth, 1, n))


def _nmm_kernel(x_ref, g_ref, mod_ref, w_ref, o_ref, h_ref, *h32_ref, shift_row, perm):
    @pl.when(pl.program_id(2) == 0)
    def _():
        m = mod_ref[0]
        h = _norm_mod(x_ref[0], g_ref[...], m[shift_row:shift_row + 1], m[shift_row + 1:shift_row + 2])
        if perm == 1:
            h_ref[...] = h.astype(BF16)
        else:
            h32 = h32_ref[0]
            cs = h.shape[0] // perm
            for j in range(h32.shape[0]):
                h32[j] = h[:, j * LANES:(j + 1) * LANES]
            for r in range(perm):
                for j in range(h32.shape[0]):
                    h_ref[r * cs:(r + 1) * cs, j * LANES:(j + 1) * LANES] = (
                        h32[j, pl.ds(r, cs, stride=perm), :].astype(BF16))

    o_ref[0] = jnp.dot(h_ref[...], w_ref[...], preferred_element_type=F32).astype(o_ref.dtype)


def _norm_mod_matmul(x, g, mod, w, *, shift_row, out_dtype, tm=1024, tn=512, perm=1):
    bsz, t, d = x.shape
    n = w.shape[1]
    tn = min(tn, n)
    return pl.pallas_call(
        functools.partial(_nmm_kernel, shift_row=shift_row, perm=perm),
        grid=(bsz, t // tm, n // tn),
        in_specs=[
            pl.BlockSpec((1, tm, d), lambda b, i, j: (b, i, 0)),
            pl.BlockSpec((1, d), lambda b, i, j: (0, 0)),
            pl.BlockSpec((1, 6, d), lambda b, i, j: (b, 0, 0)),
            pl.BlockSpec((d, tn), lambda b, i, j: (0, j)),
        ],
        out_specs=pl.BlockSpec((1, tm, tn), lambda b, i, j: (b, i, j)),
        out_shape=jax.ShapeDtypeStruct((bsz, t, n), out_dtype),
        scratch_shapes=[pltpu.VMEM((tm, d), BF16)] + ([pltpu.VMEM((d // LANES, tm, LANES), F32)] if perm > 1 else []),
        compiler_params=_params(("arbitrary", "arbitrary", "arbitrary")),
        name="norm_mod_matmul",
    )(x, g.reshape(1, d), mod, w)


def _split3(x):
    hi = x.astype(BF16)
    r = x - hi.astype(F32)
    mid = r.astype(BF16)
    lo = (r - mid.astype(F32)).astype(BF16)
    return hi, mid, lo


def _chunk_rows(x, lo, hi):
    n_sub = x.shape[0] // HGRN_CHUNK
    return jnp.concatenate([x[s * HGRN_CHUNK + lo:s * HGRN_CHUNK + hi] for s in range(n_sub)], axis=0)


def _chunk_pad(x, rows, before):
    n_sub = HGRN_TILE // HGRN_CHUNK
    pad = HGRN_CHUNK - rows
    if pad == 0:
        return x
    z = jnp.zeros((pad, x.shape[1]), x.dtype)
    parts = []
    for s in range(n_sub):
        blk = x[s * rows:(s + 1) * rows]
        parts += [z, blk] if before else [blk, z]
    return jnp.concatenate(parts, axis=0)


def _hgrn_kernel(q_ref, ff_ref, fb_ref, v_ref, g_ref, lb_ref, ng_ref, o_ref, acc_scr):
    t = q_ref.shape[1]
    c = HGRN_CHUNK
    tl = HGRN_TILE
    n_sub = tl // c
    n_tiles = t // tl
    sub = 8
    log2e = 1.4426950408889634

    lb = lb_ref[...]
    lb_floor = jnp.maximum(lb, LB_FLOOR)
    one_m_lb = 1.0 - lb
    log_one_m_lb = jnp.log(one_m_lb)

    row = lax.broadcasted_iota(jnp.int32, (tl, tl), 0)
    col = lax.broadcasted_iota(jnp.int32, (tl, tl), 1)
    tri = jnp.where(col <= row, 1.0, 0.0).astype(BF16)
    r2 = lax.broadcasted_iota(jnp.int32, (2 * tl, 2 * tl), 0) < tl
    c2 = lax.broadcasted_iota(jnp.int32, (2 * tl, 2 * tl), 1) < tl
    ones_bd = jnp.where(r2 == c2, 1.0, 0.0).astype(BF16)

    def gates(fz):
        e = jnp.exp(-jnp.abs(fz))
        r = 1.0 / (1.0 + e)
        pos = fz >= 0
        sig_pos = jnp.where(pos, r, e * r)
        sig_neg = jnp.where(pos, e * r, r)
        log_sig_neg = -(jnp.maximum(fz, 0.0) + jnp.log(1.0 + e))
        return jnp.log(lb_floor + one_m_lb * sig_pos), one_m_lb * sig_neg, log_one_m_lb + log_sig_neg

    acc_scr[...] = jnp.zeros_like(acc_scr)

    def body(n, carry):
        st_f, st_b = carry
        tf = pl.multiple_of(n * tl, tl)
        tb = pl.multiple_of((n_tiles - 1 - n) * tl, tl)
        q_f = _silu(q_ref[0, pl.ds(tf, tl), :])
        q_b = _silu(q_ref[0, pl.ds(tb, tl), :])
        v_f = v_ref[0, pl.ds(tf, tl), :]
        v_b = v_ref[0, pl.ds(tb, tl), :]
        lf_f, kk_f, lk_f = gates(ff_ref[0, pl.ds(tf, tl), :])
        lf_b, kk_b, lk_b = gates(fb_ref[0, pl.ds(tb, tl), :])

        pieces = jnp.concatenate(_split3(lf_f) + _split3(lf_b), axis=1)
        ps = jnp.dot(tri, pieces, preferred_element_type=F32)
        b_f = ps[:, 0:A_DK] + ps[:, A_DK:2 * A_DK] + ps[:, 2 * A_DK:3 * A_DK]
        p_b = ps[:, 3 * A_DK:4 * A_DK] + ps[:, 4 * A_DK:5 * A_DK] + ps[:, 5 * A_DK:6 * A_DK]
        b_b = p_b[tl - 1:tl] - p_b + lf_b

        b2_f, w2_f = b_f * log2e, (lk_f - b_f) * log2e
        b2_b, w2_b = b_b * log2e, (lk_b - b_b) * log2e
        acc_f = [None] * n_sub
        acc_b = [None] * n_sub
        weights, values = [], []
        for e in range(sub):
            if e == 0:
                wf_e, vf_e, wb_e, vb_e = w2_f, v_f, w2_b, v_b
            else:
                wf_e, vf_e = pltpu.roll(w2_f, e, 0), pltpu.roll(v_f, e, 0)
                wb_e, vb_e = pltpu.roll(w2_b, tl - e, 0), pltpu.roll(v_b, tl - e, 0)
            for a in range(c // sub):
                lo = a * sub
                rows = c - lo
                pf = _chunk_rows(q_f, lo, c) * jnp.exp2(_chunk_rows(b2_f, lo, c) + _chunk_rows(wf_e, 0, rows))
                pb = _chunk_rows(q_b, 0, rows) * jnp.exp2(_chunk_rows(b2_b, 0, rows) + _chunk_rows(wb_e, lo, c))
                if e > 0:
                    idx = lax.broadcasted_iota(jnp.int32, (n_sub * rows, 1), 0) % rows
                    pf = jnp.where(idx >= e, pf, 0.0)
                    pb = jnp.where(idx < rows - e, pb, 0.0)
                weights.append(jnp.concatenate([pf, pb], axis=1).astype(BF16))
                values.append((a, _chunk_rows(vf_e, 0, rows), _chunk_rows(vb_e, lo, c)))
        half = len(weights) // 2
        atts = []
        for part in (weights[:half], weights[half:]):
            att = jnp.dot(jnp.concatenate(part, axis=0), ones_bd, preferred_element_type=F32)
            r0 = 0
            for w in part:
                atts.append(att[r0:r0 + w.shape[0]])
                r0 += w.shape[0]
        for att, (a, vf_blk, vb_blk) in zip(atts, values):
            cf = att[:, :A_DK] * vf_blk
            cbk = att[:, A_DK:] * vb_blk
            acc_f[a] = cf if acc_f[a] is None else acc_f[a] + cf
            acc_b[a] = cbk if acc_b[a] is None else acc_b[a] + cbk
        o_f = _chunk_pad(acc_f[0], c, True)
        o_b = _chunk_pad(acc_b[0], c, False)
        for a in range(1, c // sub):
            o_f = o_f + _chunk_pad(acc_f[a], c - a * sub, True)
            o_b = o_b + _chunk_pad(acc_b[a], c - a * sub, False)

        vbf_f, vbf_b = v_f.astype(BF16), v_b.astype(BF16)
        mid_f = [jnp.zeros((c, A_DK), F32)]
        for i in range(1, n_sub):
            b_r = b_f[i * c - 1:i * c]
            qp = (q_f[i * c:(i + 1) * c] * jnp.exp(b_f[i * c:(i + 1) * c] - b_r)).astype(BF16)
            kp = (kk_f[:i * c] * jnp.exp(b_r - b_f[:i * c])).astype(BF16)
            a_i = lax.dot_general(qp, kp, (((1,), (1,)), ((), ())), preferred_element_type=F32)
            mid_f.append(jnp.dot(a_i.astype(BF16), vbf_f[:i * c], preferred_element_type=F32))
        mid_b = []
        for i in range(n_sub - 1):
            b_r = b_b[(i + 1) * c:(i + 1) * c + 1]
            qp = (q_b[i * c:(i + 1) * c] * jnp.exp(b_b[i * c:(i + 1) * c] - b_r)).astype(BF16)
            kp = (kk_b[(i + 1) * c:] * jnp.exp(b_r - b_b[(i + 1) * c:])).astype(BF16)
            a_i = lax.dot_general(qp, kp, (((1,), (1,)), ((), ())), preferred_element_type=F32)
            mid_b.append(jnp.dot(a_i.astype(BF16), vbf_b[(i + 1) * c:], preferred_element_type=F32))
        mid_b.append(jnp.zeros((c, A_DK), F32))
        o_f = o_f + jnp.concatenate(mid_f, axis=0)
        o_b = o_b + jnp.concatenate(mid_b, axis=0)

        def inter(q, v_bf, kk, b, edge, st):
            b_edge = b[edge:edge + 1]
            qb = (q * jnp.exp(b)).astype(BF16)
            o = lax.dot_general(qb, st.astype(BF16), (((1,), (1,)), ((), ())), preferred_element_type=F32)
            kd = (kk * jnp.exp(b_edge - b)).astype(BF16)
            kv = lax.dot_general(v_bf, kd, (((0,), (0,)), ((), ())), preferred_element_type=F32)
            return o, st * jnp.exp(b_edge) + kv

        oi_f, st_f = inter(q_f, vbf_f, kk_f, b_f, tl - 1, st_f)
        oi_b, st_b = inter(q_b, vbf_b, kk_b, b_b, 0, st_b)
        acc_scr[pl.ds(tf, tl), :] += o_f + oi_f
        acc_scr[pl.ds(tb, tl), :] += o_b + oi_b
        return st_f, st_b

    zero = jnp.zeros((A_DK, A_DK), F32)
    lax.fori_loop(0, n_tiles, body, (zero, zero))

    ng = ng_ref[...]

    def out_body(i, carry):
        t0 = pl.multiple_of(i * tl, tl)
        o = acc_scr[pl.ds(t0, tl), :]
        o = o * lax.rsqrt(jnp.mean(o * o, axis=-1, keepdims=True) + RMS_EPS) * ng
        o_ref[0, pl.ds(t0, tl), :] = (o * _silu(g_ref[0, pl.ds(t0, tl), :])).astype(o_ref.dtype)
        return carry

    lax.fori_loop(0, n_tiles, out_body, 0)


def _hgrn2(z, lb, norm_g):
    bsz, t, w5 = z.shape
    heads = w5 // 5 // A_DK

    def col(k):
        return pl.BlockSpec((1, t, A_DK), lambda b, h, k=k: (b, 0, h + heads * k))

    return pl.pallas_call(
        _hgrn_kernel,
        grid=(bsz, heads),
        in_specs=[col(0), col(1), col(2), col(3), col(4),
                  pl.BlockSpec((1, A_DK), lambda b, h: (0, h)),
                  pl.BlockSpec((1, A_DK), lambda b, h: (0, 0))],
        out_specs=pl.BlockSpec((1, t, A_DK), lambda b, h: (b, 0, h)),
        out_shape=jax.ShapeDtypeStruct((bsz, t, heads * A_DK), BF16),
        scratch_shapes=[pltpu.VMEM((t, A_DK), F32)],
        compiler_params=_params(("arbitrary", "arbitrary")),
        name="hgrn2",
    )(z, z, z, z, z, lb.reshape(1, heads * A_DK), norm_g.reshape(1, A_DK))


def _stacked_attention(q, k, v, bias, extra_ok):
    m_rows = q.shape[0]
    heads = q.shape[1] // ATT_DH
    head_of_lane = lax.broadcasted_iota(jnp.int32, (1, q.shape[1]), 1) // ATT_DH
    zero = jnp.zeros((), q.dtype)
    q_bd = jnp.concatenate([jnp.where(head_of_lane == h, q, zero) for h in range(heads)], axis=0)
    s = lax.dot_general(q_bd, k, (((1,), (1,)), ((), ())), preferred_element_type=F32) + bias
    if extra_ok is not None:
        s = jnp.where(extra_ok, s, MASK_VALUE)
    m = jnp.max(s, axis=-1, keepdims=True)
    p = jnp.exp(s - m)
    den = jnp.sum(p, axis=-1, keepdims=True)
    o = jnp.dot(p.astype(BF16), v, preferred_element_type=F32) / den
    lse = m + jnp.log(den)
    out = o[0:m_rows]
    lse_b = jnp.broadcast_to(lse[0:m_rows], out.shape)
    for h in range(1, heads):
        sel = head_of_lane == h
        out = jnp.where(sel, o[h * m_rows:(h + 1) * m_rows], out)
        lse_b = jnp.where(sel, lse[h * m_rows:(h + 1) * m_rows], lse_b)
    return out, lse_b


def _na_kernel(q_ref, k_ref, v_ref, bias_ref, o_ref, *, rows, kr):
    r = pl.program_id(1)
    rs = jnp.clip(r - kr // 2, 0, rows - kr)
    k0 = pl.multiple_of(rs * GRID_W, GRID_W)
    nk = kr * GRID_W
    scale = ATT_DH ** -0.5
    q = q_ref[0] * scale
    for s in range(q.shape[-1] // ATT_SLAB):
        sl = slice(s * ATT_SLAB, (s + 1) * ATT_SLAB)
        nb = ATT_HPS * GRID_W
        out, _ = _stacked_attention(q[:, sl], k_ref[0, pl.ds(k0, nk), sl], v_ref[0, pl.ds(k0, nk), sl],
                                    bias_ref[0, s * nb:(s + 1) * nb, :], None)
        o_ref[0, :, sl] = out.astype(o_ref.dtype)


def _na_bias_table(rpb, rows):
    heads = rpb.shape[0]
    kr = min(NA_KR, rows)
    col = np.arange(GRID_W)
    cs = np.clip(col - NA_KC // 2, 0, GRID_W - NA_KC)
    kc = np.arange(GRID_W)
    rel = kc[None, :] - col[:, None] + NA_KC - 1
    ok = (kc[None, :] >= cs[:, None]) & (kc[None, :] < cs[:, None] + NA_KC)
    rel_c = np.clip(rel, 0, 2 * NA_KC - 2)
    full = jnp.where(ok[None, None], rpb.astype(F32)[:, :, rel_c], MASK_VALUE)
    variants = []
    for var in range(kr):
        di0 = NA_KR - 1 - var
        blk = full[:, di0:di0 + kr]
        variants.append(jnp.transpose(blk, (0, 2, 1, 3)).reshape(heads * GRID_W, kr * GRID_W))
    return jnp.stack(variants)


def _neighbourhood_attention(zb, rpb):
    bsz, t, w3 = zb.shape
    width = w3 // 3
    heads = width // ATT_DH
    rows = t // GRID_W
    kr = min(NA_KR, rows)
    table = _na_bias_table(rpb, rows)

    def var_of(r):
        return r - jnp.clip(r - kr // 2, 0, rows - kr)

    return pl.pallas_call(
        functools.partial(_na_kernel, rows=rows, kr=kr),
        grid=(bsz, rows),
        in_specs=[
            pl.BlockSpec((1, GRID_W, width), lambda b, r: (b, r, 0)),
            pl.BlockSpec((1, t, width), lambda b, r: (b, 0, 1)),
            pl.BlockSpec((1, t, width), lambda b, r: (b, 0, 2)),
            pl.BlockSpec((1, heads * GRID_W, kr * GRID_W), lambda b, r: (var_of(r), 0, 0)),
        ],
        out_specs=pl.BlockSpec((1, GRID_W, width), lambda b, r: (b, r, 0)),
        out_shape=jax.ShapeDtypeStruct((bsz, t, width), BF16),
        compiler_params=_params(("arbitrary", "arbitrary")),
        name="neighbourhood_attention",
    )(zb, zb, zb, table)


def _dil_kernel(q_ref, kp_ref, k_ref, kn_ref, vp_ref, v_ref, vn_ref, bias_ref, o_ref, lse_ref,
                k_scr, v_scr, *, seq_len):
    i = pl.program_id(1)
    dil, cs = q_ref.shape[2], q_ref.shape[3]
    width = q_ref.shape[4]
    scale = ATT_DH ** -0.5
    nk = ATT_SUB + 2 * ATT_HALO
    nb = ATT_HPS * ATT_SUB
    kcol = lax.broadcasted_iota(jnp.int32, (1, nk), 1)
    for r in range(dil):
        k_scr[0:ATT_HALO] = kp_ref[0, 0, r]
        k_scr[ATT_HALO:ATT_HALO + cs] = k_ref[0, 0, r]
        k_scr[ATT_HALO + cs:] = kn_ref[0, 0, r]
        v_scr[0:ATT_HALO] = vp_ref[0, 0, r]
        v_scr[ATT_HALO:ATT_HALO + cs] = v_ref[0, 0, r]
        v_scr[ATT_HALO + cs:] = vn_ref[0, 0, r]
        for sb in range(cs // ATT_SUB):
            r0 = sb * ATT_SUB
            kpos = i * cs + r0 - ATT_HALO + kcol
            ok = (kpos >= 0) & (kpos < seq_len)
            q = q_ref[0, 0, r, r0:r0 + ATT_SUB, :] * scale
            if dil == 1:
                rows_out = pl.ds(r0, ATT_SUB)
            else:
                rows_out = pl.ds(r0 * dil + r, ATT_SUB, stride=dil)
            for s in range(width // ATT_SLAB):
                sl = slice(s * ATT_SLAB, (s + 1) * ATT_SLAB)
                out, lse = _stacked_attention(q[:, sl], k_scr[r0:r0 + nk, sl], v_scr[r0:r0 + nk, sl],
                                              bias_ref[s * nb:(s + 1) * nb, :], ok)
                for j in range(ATT_SLAB // LANES):
                    ch = s * (ATT_SLAB // LANES) + j
                    o_ref[0, ch, rows_out, :] = out[:, j * LANES:(j + 1) * LANES]
                    lse_ref[0, ch, rows_out, :] = lse[:, j * LANES:(j + 1) * LANES]


def _alibi_band(slopes_g, dil):
    rel = np.arange(ATT_SUB + 2 * ATT_HALO)[None, :] - ATT_HALO - np.arange(ATT_SUB)[:, None]
    band = np.abs(rel) <= ATT_HALO
    dist = jnp.asarray(np.abs(rel) * dil, F32)
    bias = jnp.where(band[None], -slopes_g[:, None, None] * dist[None], MASK_VALUE)
    return bias.reshape(slopes_g.shape[0] * ATT_SUB, ATT_SUB + 2 * ATT_HALO)


def _dilated_group(zp, dil, slopes_g, tm):
    bsz, t, f = zp.shape
    width = f // 3
    cs = tm // dil
    nt = t // tm
    zv = zp.reshape(bsz, nt, dil, cs, f)
    bias = _alibi_band(slopes_g, dil)
    tail = cs // ATT_HALO - 1

    def main(off):
        return pl.BlockSpec((1, 1, dil, cs, width), lambda b, i: (b, i, 0, 0, off))

    def prev(off):
        return pl.BlockSpec((1, 1, dil, ATT_HALO, width), lambda b, i: (b, jnp.maximum(i - 1, 0), 0, tail, off))

    def nxt(off):
        return pl.BlockSpec((1, 1, dil, ATT_HALO, width), lambda b, i: (b, jnp.minimum(i + 1, nt - 1), 0, 0, off))

    out_spec = pl.BlockSpec((1, width // LANES, tm, LANES), lambda b, i: (b, 0, i, 0))
    return pl.pallas_call(
        functools.partial(_dil_kernel, seq_len=t // dil),
        grid=(bsz, nt),
        in_specs=[main(0), prev(1), main(1), nxt(1), prev(2), main(2), nxt(2),
                  pl.BlockSpec(bias.shape, lambda b, i: (0, 0))],
        out_specs=[out_spec, out_spec],
        out_shape=[jax.ShapeDtypeStruct((bsz, width // LANES, t, LANES), F32)] * 2,
        scratch_shapes=[pltpu.VMEM((cs + 2 * ATT_HALO, width), BF16)] * 2,
        compiler_params=_params(("arbitrary", "arbitrary")),
        name=f"dilated_attention_d{dil}",
    )(zv, zv, zv, zv, zv, zv, zv, bias)


def _pool_kernel(up_ref, u_ref, un_ref, w_ref, sc_ref, o_ref, scr, *, seq_len):
    i = pl.program_id(1)
    tm = u_ref.shape[1]
    gc = w_ref.shape[1]
    scr[0:HALO_ROWS] = jnp.where(i > 0, up_ref[0], 0.0)
    scr[HALO_ROWS:HALO_ROWS + tm] = u_ref[0]
    scr[HALO_ROWS + tm:] = jnp.where(i < pl.num_programs(1) - 1, un_ref[0], 0.0)
    tpos = i * tm + lax.broadcasted_iota(jnp.int32, (tm, 1), 0)
    for g, w in enumerate(POOL_WINDOWS):
        sl = slice(g * gc, (g + 1) * gc)
        tot = jnp.zeros((tm, gc), F32)
        for off in range(-(w // 2), w - w // 2):
            tot = tot + scr[HALO_ROWS + off:HALO_ROWS + off + tm, sl]
        lo = jnp.clip(tpos - w // 2, 0, seq_len)
        hi = jnp.clip(tpos + w - w // 2, 0, seq_len)
        mean = tot / (hi - lo).astype(F32)
        diff = mean - scr[HALO_ROWS:HALO_ROWS + tm, sl]
        y = jnp.dot(diff.astype(BF16), w_ref[g], preferred_element_type=F32)
        o_ref[0, :, sl] = (y * sc_ref[:, sl]).astype(o_ref.dtype)


def _multiscale_pool(u, w_groups, scale, tm=512):
    bsz, t, ch = u.shape
    hb = tm // HALO_ROWS
    last = t // HALO_ROWS - 1
    return pl.pallas_call(
        functools.partial(_pool_kernel, seq_len=t),
        grid=(bsz, t // tm),
        in_specs=[
            pl.BlockSpec((1, HALO_ROWS, ch), lambda b, i: (b, jnp.maximum(i * hb - 1, 0), 0)),
            pl.BlockSpec((1, tm, ch), lambda b, i: (b, i, 0)),
            pl.BlockSpec((1, HALO_ROWS, ch), lambda b, i: (b, jnp.minimum((i + 1) * hb, last), 0)),
            pl.BlockSpec(w_groups.shape, lambda b, i: (0, 0, 0)),
            pl.BlockSpec((1, ch), lambda b, i: (0, 0)),
        ],
        out_specs=pl.BlockSpec((1, tm, ch), lambda b, i: (b, i, 0)),
        out_shape=jax.ShapeDtypeStruct((bsz, t, ch), BF16),
        scratch_shapes=[pltpu.VMEM((tm + 2 * HALO_ROWS, ch), F32)],
        compiler_params=_params(("arbitrary", "arbitrary")),
        name="multiscale_pool",
    )(u, u, u, w_groups.astype(BF16), scale.reshape(1, ch))


def _out_even_kernel(oa_ref, ob_ref, wa_ref, wb_ref, x_ref, mod_ref, o_ref, *, gate_row):
    mixed = jnp.dot(oa_ref[0], wa_ref[...], preferred_element_type=F32)
    mixed = mixed + jnp.dot(ob_ref[0], wb_ref[...], preferred_element_type=F32)
    gate = mod_ref[0][gate_row:gate_row + 1]
    o_ref[0] = x_ref[0] + gate * mixed


def _out_odd_kernel(o0_ref, o1_ref, o2_ref, l0_ref, l1_ref, l2_ref, od_ref, wa_ref, wb_ref, x_ref, mod_ref,
                    o_ref, *, gate_row):
    chunks = []
    for j in range(o0_ref.shape[1]):
        l0, l1, l2 = l0_ref[0, j], l1_ref[0, j], l2_ref[0, j]
        m = jnp.maximum(jnp.maximum(l0, l1), l2)
        e0, e1, e2 = jnp.exp(l0 - m), jnp.exp(l1 - m), jnp.exp(l2 - m)
        chunks.append((e0 * o0_ref[0, j] + e1 * o1_ref[0, j] + e2 * o2_ref[0, j]) / (e0 + e1 + e2))
    oc = jnp.concatenate(chunks, axis=1)
    mixed = jnp.dot(oc.astype(BF16), wa_ref[...], preferred_element_type=F32)
    mixed = mixed + jnp.dot(od_ref[0], wb_ref[...], preferred_element_type=F32)
    gate = mod_ref[0][gate_row:gate_row + 1]
    o_ref[0] = x_ref[0] + gate * mixed


def _out_proj(parts, w_out, x, mod, *, gate_row, odd, tm=512):
    bsz, t, d = x.shape
    half = w_out.shape[0] // 2
    wa, wb = w_out[:half].astype(BF16), w_out[half:].astype(BF16)
    tok = lambda width: pl.BlockSpec((1, tm, width), lambda b, i: (b, i, 0))
    chunked = pl.BlockSpec((1, half // LANES, tm, LANES), lambda b, i: (b, 0, i, 0))
    wspec = pl.BlockSpec((half, d), lambda b, i: (0, 0))
    kern = _out_odd_kernel if odd else _out_even_kernel
    part_specs = [chunked if p.ndim == 4 else tok(half) for p in parts]
    return pl.pallas_call(
        functools.partial(kern, gate_row=gate_row),
        grid=(bsz, t // tm),
        in_specs=part_specs + [wspec, wspec, tok(d), pl.BlockSpec((1, 6, d), lambda b, i: (b, 0, 0))],
        out_specs=tok(d),
        out_shape=jax.ShapeDtypeStruct((bsz, t, d), F32),
        compiler_params=_params(("arbitrary", "arbitrary")),
        name="out_proj_odd" if odd else "out_proj_even",
    )(*parts, wa, wb, x, mod)


def _ffn_up_kernel(xp_ref, x_ref, xn_ref, g_ref, mod_ref, wg_ref, wv_ref, cw_ref, cb_ref, o_ref,
                   h_scr, gate_scr, *, shift_row):
    i = pl.program_id(1)
    tm = x_ref.shape[1]

    @pl.when(pl.program_id(2) == 0)
    def _():
        m = mod_ref[0]
        shift, scale = m[shift_row:shift_row + 1], m[shift_row + 1:shift_row + 2]
        g = g_ref[...]
        h_scr[0:HALO_ROWS] = _norm_mod(xp_ref[0], g, shift, scale).astype(BF16)
        h_scr[HALO_ROWS:HALO_ROWS + tm] = _norm_mod(x_ref[0], g, shift, scale).astype(BF16)
        h_scr[HALO_ROWS + tm:] = _norm_mod(xn_ref[0], g, shift, scale).astype(BF16)

    gate = jnp.dot(h_scr[...], wg_ref[...], preferred_element_type=F32)
    val = jnp.dot(h_scr[HALO_ROWS:HALO_ROWS + tm], wv_ref[...], preferred_element_type=F32)
    row = lax.broadcasted_iota(jnp.int32, (tm + 2 * HALO_ROWS, 1), 0)
    inside = ((row >= HALO_ROWS) | (i > 0)) & ((row < HALO_ROWS + tm) | (i < pl.num_programs(1) - 1))
    gate_scr[...] = jnp.where(inside, gate, 0.0)
    cw = cw_ref[...]
    conv = cb_ref[...]
    for j in range(FFN_CONV):
        a = HALO_ROWS + j - FFN_CONV // 2
        conv = conv + gate_scr[a:a + tm, :] * cw[j:j + 1]
    o_ref[0] = (_silu(conv) * val).astype(o_ref.dtype)


def _ffn_up(x, g, mod, w_up, conv_w, conv_b, *, shift_row, tm=512, tn=1408):
    bsz, t, d = x.shape
    dff = w_up.shape[1] // 2
    nj = dff // tn
    hb = tm // HALO_ROWS
    last = t // HALO_ROWS - 1
    wu = w_up.astype(BF16)
    return pl.pallas_call(
        functools.partial(_ffn_up_kernel, shift_row=shift_row),
        grid=(bsz, t // tm, nj),
        in_specs=[
            pl.BlockSpec((1, HALO_ROWS, d), lambda b, i, j: (b, jnp.maximum(i * hb - 1, 0), 0)),
            pl.BlockSpec((1, tm, d), lambda b, i, j: (b, i, 0)),
            pl.BlockSpec((1, HALO_ROWS, d), lambda b, i, j: (b, jnp.minimum((i + 1) * hb, last), 0)),
            pl.BlockSpec((1, d), lambda b, i, j: (0, 0)),
            pl.BlockSpec((1, 6, d), lambda b, i, j: (b, 0, 0)),
            pl.BlockSpec((d, tn), lambda b, i, j: (0, j)),
            pl.BlockSpec((d, tn), lambda b, i, j: (0, j + nj)),
            pl.BlockSpec((FFN_CONV, tn), lambda b, i, j: (0, j)),
            pl.BlockSpec((1, tn), lambda b, i, j: (0, j)),
        ],
        out_specs=pl.BlockSpec((1, tm, tn), lambda b, i, j: (b, i, j)),
        out_shape=jax.ShapeDtypeStruct((bsz, t, dff), BF16),
        scratch_shapes=[pltpu.VMEM((tm + 2 * HALO_ROWS, d), BF16),
                        pltpu.VMEM((tm + 2 * HALO_ROWS, tn), F32)],
        compiler_params=_params(("arbitrary", "arbitrary", "arbitrary")),
        name="ffn_up_conv",
    )(x, x, x, g.reshape(1, d), mod, wu, wu, conv_w, conv_b.reshape(1, dff))


def _ffn_down_kernel(a_ref, w_ref, x_ref, mod_ref, o_ref, *, gate_row):
    y = jnp.dot(a_ref[0], w_ref[...], preferred_element_type=F32)
    o_ref[0] = x_ref[0] + mod_ref[0][gate_row:gate_row + 1] * y


def _ffn_down(a, w_down, x, mod, *, gate_row, tm=512):
    bsz, t, d = x.shape
    dff = a.shape[-1]
    return pl.pallas_call(
        functools.partial(_ffn_down_kernel, gate_row=gate_row),
        grid=(bsz, t // tm),
        in_specs=[
            pl.BlockSpec((1, tm, dff), lambda b, i: (b, i, 0)),
            pl.BlockSpec((dff, d), lambda b, i: (0, 0)),
            pl.BlockSpec((1, tm, d), lambda b, i: (b, i, 0)),
            pl.BlockSpec((1, 6, d), lambda b, i: (b, 0, 0)),
        ],
        out_specs=pl.BlockSpec((1, tm, d), lambda b, i: (b, i, 0)),
        out_shape=jax.ShapeDtypeStruct((bsz, t, d), F32),
        compiler_params=_params(("arbitrary", "arbitrary")),
        name="ffn_down",
    )(a, w_down.astype(BF16), x, mod)


def _final_norm_kernel(x_ref, g_ref, o_ref):
    x = x_ref[0]
    o_ref[0] = x * lax.rsqrt(jnp.mean(x * x, axis=-1, keepdims=True) + RMS_EPS) * g_ref[...]


def _final_norm(x, g, tm=1024):
    bsz, t, d = x.shape
    return pl.pallas_call(
        _final_norm_kernel,
        grid=(bsz, t // tm),
        in_specs=[pl.BlockSpec((1, tm, d), lambda b, i: (b, i, 0)), pl.BlockSpec((1, d), lambda b, i: (0, 0))],
        out_specs=pl.BlockSpec((1, tm, d), lambda b, i: (b, i, 0)),
        out_shape=jax.ShapeDtypeStruct((bsz, t, d), F32),
        compiler_params=_params(("arbitrary", "arbitrary")),
        name="final_norm",
    )(x, g.reshape(1, d))


def kernel(x, c, ada_w, ada_b, norm_mix_g, norm_ffn_g, even_w_in, even_w_out, hgrn_lb_logits, hgrn_norm_g, na_rpb, odd_w_in, odd_w_out, pool_w, pool_scale, ffn_w_up, ffn_conv_w, ffn_conv_b, ffn_w_down, final_norm_g):
    bsz, t, d = x.shape
    depth = ada_w.shape[0]
    a_width = hgrn_lb_logits.shape[1]
    n_groups = len(C_CONFIGS)
    c_width = C_HPG * ATT_DH
    n_c = 3 * n_groups * c_width

    lb_soft = jax.nn.softmax(hgrn_lb_logits.astype(F32), axis=0)
    lower_bounds = jnp.cumsum(lb_soft, axis=0) - lb_soft[0]
    slopes = jnp.exp2(-8.0 * jnp.arange(1, n_groups * C_HPG + 1, dtype=F32) / (n_groups * C_HPG))
    slopes = slopes.reshape(n_groups, C_HPG)

    mod_all = _ada_mod(c, ada_w, ada_b).reshape(depth, bsz, 6, d)

    for l in range(depth):
        mod = mod_all[l]
        if l % 2 == 0:
            e = l // 2
            w_in = even_w_in[e].astype(BF16)
            z_a = _norm_mod_matmul(x, norm_mix_g[l], mod, w_in[:, :5 * a_width], shift_row=0, out_dtype=F32)
            z_b = _norm_mod_matmul(x, norm_mix_g[l], mod, w_in[:, 5 * a_width:], shift_row=0, out_dtype=BF16)
            o_a = _hgrn2(z_a, lower_bounds[e], hgrn_norm_g[e])
            o_b = _neighbourhood_attention(z_b, na_rpb[e])
            x = _out_proj([o_a, o_b], even_w_out[e], x, mod, gate_row=2, odd=False)
        else:
            o_i = l // 2
            w_in = odd_w_in[o_i].astype(BF16)
            z_d = _norm_mod_matmul(x, norm_mix_g[l], mod, w_in[:, n_c:], shift_row=0, out_dtype=F32)
            outs, lses = [], []
            for g, (_, dil) in enumerate(C_CONFIGS):
                w_g = jnp.concatenate([w_in[:, (k * n_groups + g) * c_width:(k * n_groups + g + 1) * c_width]
                                       for k in range(3)], axis=1)
                z_g = _norm_mod_matmul(x, norm_mix_g[l], mod, w_g, shift_row=0, out_dtype=BF16,
                                       tm=DIL_TILE[dil], perm=dil)
                o_g, lse_g = _dilated_group(z_g, dil, slopes[g], DIL_TILE[dil])
                outs.append(o_g)
                lses.append(lse_g)
            o_d = _multiscale_pool(z_d, pool_w[o_i], pool_scale[o_i])
            x = _out_proj(outs + lses + [o_d], odd_w_out[o_i], x, mod, gate_row=2, odd=True)
        a = _ffn_up(x, norm_ffn_g[l], mod, ffn_w_up[l], ffn_conv_w[l], ffn_conv_b[l], shift_row=3)
        x = _ffn_down(a, ffn_w_down[l], x, mod, gate_row=5)
    return _final_norm(x, final_norm_g)
```

```python
import functools

import jax
import jax.numpy as jnp
import numpy as np
from jax import lax
from jax.experimental import pallas as pl
from jax.experimental.pallas import tpu as pltpu

F32 = jnp.float32
BF16 = jnp.bfloat16

GRID_W = 64
RMS_EPS = 1e-6
LB_FLOOR = 1e-12
MASK_VALUE = -1e30
A_DK = 128
HGRN_TILE = 128
HGRN_GROUP = 8
HGRN_LEVELS = (8, 16, 32, 64)
ATT_DH = 64
ATT_SLAB = 256
ATT_HPS = ATT_SLAB // ATT_DH
NA_KR = 8
NA_KC = 16
C_CONFIGS = ((128, 1), (512, 4), (2048, 16))
C_HPG = 8
DIL_TILE = {1: 1024, 4: 1024, 16: 2048}
ATT_SUB = 128
ATT_HALO = 64
POOL_WINDOWS = (2, 4, 8, 16)
POOL_HALO = 8
FFN_CONV = 3
FFN_COLS = 256
HALO_ROWS = 8
LANES = 128
VMEM_LIMIT = 56 * 1024 * 1024


def _params(sem):
    return pltpu.CompilerParams(dimension_semantics=sem, vmem_limit_bytes=VMEM_LIMIT)


def _silu(v):
    return v * (1.0 / (1.0 + jnp.exp(-v)))


def _norm_mod(x, g, shift, scale):
    ms = jnp.mean(x * x, axis=-1, keepdims=True)
    return x * lax.rsqrt(ms + RMS_EPS) * (g * (1.0 + scale)) + shift


def _ada_kernel(c_ref, w_ref, b_ref, o_ref):
    c = c_ref[...]
    ca = _silu(c).astype(BF16)
    o_ref[0] = jnp.dot(ca, w_ref[0].astype(BF16), preferred_element_type=F32) + b_ref[0]


def _ada_mod(c, ada_w, ada_b):
    depth, d, n = ada_w.shape
    bsz = c.shape[0]
    tn = 1024
    return pl.pallas_call(
        _ada_kernel,
        grid=(depth, n // tn),
        in_specs=[
            pl.BlockSpec((bsz, d), lambda l, j: (0, 0)),
            pl.BlockSpec((1, d, tn), lambda l, j: (l, 0, j)),
            pl.BlockSpec((1, 1, tn), lambda l, j: (l, 0, j)),
        ],
        out_specs=pl.BlockSpec((1, bsz, tn), lambda l, j: (l, 0, j)),
        out_shape=jax.ShapeDtypeStruct((depth, bsz, n), F32),
        compiler_params=_params(("arbitrary", "arbitrary")),
        name="ada_mod",
    )(c, ada_w, ada_b.reshape(depth, 1, n))


def _nmm_kernel(x_ref, g_ref, mod_ref, w_ref, o_ref, h_ref, *h32_ref, shift_row, perm):
    @pl.when(pl.program_id(2) == 0)
    def _():
        m = mod_ref[0]
        h = _norm_mod(x_ref[0], g_ref[...], m[shift_row:shift_row + 1], m[shift_row + 1:shift_row + 2])
        if perm == 1:
            h_ref[...] = h.astype(BF16)
        else:
            h32 = h32_ref[0]
            cs = h.shape[0] // perm
            for j in range(h32.shape[0]):
                h32[j] = h[:, j * LANES:(j + 1) * LANES]
            for r in range(perm):
                for j in range(h32.shape[0]):
                    h_ref[r * cs:(r + 1) * cs, j * LANES:(j + 1) * LANES] = (
                        h32[j, pl.ds(r, cs, stride=perm), :].astype(BF16))

    o_ref[0] = jnp.dot(h_ref[...], w_ref[...], preferred_element_type=F32).astype(o_ref.dtype)


def _norm_mod_matmul(x, g, mod, w, *, shift_row, out_dtype, tm=1024, perm=1):
    bsz, t, d = x.shape
    n = w.shape[1]
    tn = n
    return pl.pallas_call(
        functools.partial(_nmm_kernel, shift_row=shift_row, perm=perm),
        grid=(bsz, t // tm, n // tn),
        in_specs=[
            pl.BlockSpec((1, tm, d), lambda b, i, j: (b, i, 0)),
            pl.BlockSpec((1, d), lambda b, i, j: (0, 0)),
            pl.BlockSpec((1, 6, d), lambda b, i, j: (b, 0, 0)),
            pl.BlockSpec((d, tn), lambda b, i, j: (0, j), pipeline_mode=pl.Buffered(1)),
        ],
        out_specs=pl.BlockSpec((1, tm, tn), lambda b, i, j: (b, i, j)),
        out_shape=jax.ShapeDtypeStruct((bsz, t, n), out_dtype),
        scratch_shapes=[pltpu.VMEM((tm, d), BF16)] + ([pltpu.VMEM((d // LANES, tm, LANES), F32)] if perm > 1 else []),
        compiler_params=_params(("arbitrary", "arbitrary", "arbitrary")),
        name="norm_mod_matmul",
    )(x, g.reshape(1, d), mod, w)


def _split3(x):
    hi = x.astype(BF16)
    r = x - hi.astype(F32)
    mid = r.astype(BF16)
    lo = (r - mid.astype(F32)).astype(BF16)
    return hi, mid, lo


def _span_code(later, earlier):
    g = HGRN_GROUP
    code = jnp.where(((later // g) == (earlier // g)) & (later >= earlier), later - earlier, -1)
    for li, m in enumerate(HGRN_LEVELS):
        straddles = ((later // (2 * m)) == (earlier // (2 * m))) & ((later % (2 * m)) >= m) & ((earlier % (2 * m)) < m)
        code = jnp.where(straddles, g + li, code)
    return code


def _hgrn_kernel(q_ref, ff_ref, fb_ref, v_ref, g_ref, lb_ref, ng_ref, o_ref, acc_scr, qs_scr):
    t = q_ref.shape[1]
    tl = HGRN_TILE
    g8 = HGRN_GROUP
    n_tiles = t // tl
    log2e = 1.4426950408889634

    lb = lb_ref[...]
    lb_floor = jnp.maximum(lb, LB_FLOOR)
    one_m_lb = 1.0 - lb
    log_one_m_lb = jnp.log(one_m_lb)

    row = lax.broadcasted_iota(jnp.int32, (tl, tl), 0)
    col = lax.broadcasted_iota(jnp.int32, (tl, tl), 1)
    tri = jnp.where(col <= row, 1.0, 0.0).astype(BF16)
    code_f = _span_code(row, col)
    code_b = _span_code(col, row)
    r2 = lax.broadcasted_iota(jnp.int32, (2 * tl, 2 * tl), 0) < tl
    c2 = lax.broadcasted_iota(jnp.int32, (2 * tl, 2 * tl), 1) < tl
    ones_bd = jnp.where(r2 == c2, 1.0, 0.0).astype(BF16)

    def gates(fz):
        e = jnp.exp(-jnp.abs(fz))
        r = 1.0 / (1.0 + e)
        pos = fz >= 0
        sig_pos = jnp.where(pos, r, e * r)
        sig_neg = jnp.where(pos, e * r, r)
        log_sig_neg = -(jnp.maximum(fz, 0.0) + jnp.log(1.0 + e))
        return jnp.log(lb_floor + one_m_lb * sig_pos), one_m_lb * sig_neg, log_one_m_lb + log_sig_neg

    def rot_group(x, shift):
        return pltpu.roll(x.reshape(tl // g8, g8, x.shape[1]), shift % g8, 1).reshape(x.shape)

    def span_ref(b, fwd, m):
        blocks = []
        for a in range(0, tl, 2 * m):
            edge = a + m - 1 if fwd else a + m
            blocks.append(jnp.broadcast_to(b[edge:edge + 1], (2 * m, b.shape[1])))
        return jnp.concatenate(blocks, axis=0)

    acc_scr[...] = jnp.zeros_like(acc_scr)

    def silu_body(i, carry):
        t0 = pl.multiple_of(i * tl, tl)
        qs_scr[pl.ds(t0, tl), :] = _silu(q_ref[0, pl.ds(t0, tl), :])
        return carry

    lax.fori_loop(0, n_tiles, silu_body, 0)

    def body(n, carry):
        st_f, st_b = carry
        tf = pl.multiple_of(n * tl, tl)
        tb = pl.multiple_of((n_tiles - 1 - n) * tl, tl)
        q_f = qs_scr[pl.ds(tf, tl), :]
        q_b = qs_scr[pl.ds(tb, tl), :]
        v_f = v_ref[0, pl.ds(tf, tl), :].astype(BF16)
        v_b = v_ref[0, pl.ds(tb, tl), :].astype(BF16)
        lf_f, kk_f, lk_f = gates(ff_ref[0, pl.ds(tf, tl), :])
        lf_b, kk_b, lk_b = gates(fb_ref[0, pl.ds(tb, tl), :])

        pieces = jnp.concatenate(_split3(lf_f) + _split3(lf_b), axis=1)
        ps = jnp.dot(tri, pieces, preferred_element_type=F32)
        b_f = ps[:, 0:A_DK] + ps[:, A_DK:2 * A_DK] + ps[:, 2 * A_DK:3 * A_DK]
        p_b = ps[:, 3 * A_DK:4 * A_DK] + ps[:, 4 * A_DK:5 * A_DK] + ps[:, 5 * A_DK:6 * A_DK]
        b_b = p_b[tl - 1:tl] - p_b + lf_b

        b2_f, w2_f = b_f * log2e, (lk_f - b_f) * log2e
        b2_b, w2_b = b_b * log2e, (lk_b - b_b) * log2e
        lhs = []
        for e in range(g8):
            wf_e = w2_f if e == 0 else rot_group(w2_f, e)
            wb_e = w2_b if e == 0 else rot_group(w2_b, g8 - e)
            pf = q_f * jnp.exp2(jnp.minimum(b2_f + wf_e, 0.0))
            pb = q_b * jnp.exp2(jnp.minimum(b2_b + wb_e, 0.0))
            lhs.append(jnp.concatenate([pf, pb], axis=1).astype(BF16))
        half = g8 // 2
        att = [jnp.dot(jnp.concatenate(part, axis=0), ones_bd, preferred_element_type=F32)
               for part in (lhs[:half], lhs[half:])]
        a_f = jnp.zeros((tl, tl), F32)
        a_b = jnp.zeros((tl, tl), F32)
        for e in range(g8):
            blk = att[e // half][(e % half) * tl:(e % half + 1) * tl]
            a_f = jnp.where(code_f == e, blk[:, :A_DK], a_f)
            a_b = jnp.where(code_b == e, blk[:, A_DK:], a_b)

        for li, m in enumerate(HGRN_LEVELS):
            e_f = jnp.exp(-jnp.abs(b_f - span_ref(b_f, True, m)))
            e_b = jnp.exp(-jnp.abs(b_b - span_ref(b_b, False, m)))
            s_f = lax.dot_general((q_f * e_f).astype(BF16), (kk_f * e_f).astype(BF16),
                                  (((1,), (1,)), ((), ())), preferred_element_type=F32)
            s_b = lax.dot_general((q_b * e_b).astype(BF16), (kk_b * e_b).astype(BF16),
                                  (((1,), (1,)), ((), ())), preferred_element_type=F32)
            a_f = jnp.where(code_f == g8 + li, s_f, a_f)
            a_b = jnp.where(code_b == g8 + li, s_b, a_b)

        def inter(q, v_bf, kk, b, edge, st):
            b_edge = b[edge:edge + 1]
            qb = (q * jnp.exp(b)).astype(BF16)
            o = lax.dot_general(qb, st.astype(BF16), (((1,), (1,)), ((), ())), preferred_element_type=F32)
            kd = (kk * jnp.exp(b_edge - b)).astype(BF16)
            kv = lax.dot_general(v_bf, kd, (((0,), (0,)), ((), ())), preferred_element_type=F32)
            return o, st * jnp.exp(b_edge) + kv

        oi_f, st_f = inter(q_f, v_f, kk_f, b_f, tl - 1, st_f)
        oi_b, st_b = inter(q_b, v_b, kk_b, b_b, 0, st_b)
        acc_scr[pl.ds(tf, tl), :] += jnp.dot(a_f.astype(BF16), v_f, preferred_element_type=F32) + oi_f
        acc_scr[pl.ds(tb, tl), :] += jnp.dot(a_b.astype(BF16), v_b, preferred_element_type=F32) + oi_b
        return st_f, st_b

    zero = jnp.zeros((A_DK, A_DK), F32)
    lax.fori_loop(0, n_tiles, body, (zero, zero))

    ng = ng_ref[...]

    def out_body(i, carry):
        t0 = pl.multiple_of(i * tl, tl)
        o = acc_scr[pl.ds(t0, tl), :]
        o = o * lax.rsqrt(jnp.mean(o * o, axis=-1, keepdims=True) + RMS_EPS) * ng
        o_ref[0, pl.ds(t0, tl), :] = (o * _silu(g_ref[0, pl.ds(t0, tl), :])).astype(o_ref.dtype)
        return carry

    lax.fori_loop(0, n_tiles, out_body, 0)


def _hgrn2(z, lb, norm_g):
    bsz, t, w5 = z.shape
    heads = w5 // 5 // A_DK

    def col(k):
        return pl.BlockSpec((1, t, A_DK), lambda b, h, k=k: (b, 0, h + heads * k))

    return pl.pallas_call(
        _hgrn_kernel,
        grid=(bsz, heads),
        in_specs=[col(0), col(1), col(2), col(3), col(4),
                  pl.BlockSpec((1, A_DK), lambda b, h: (0, h)),
                  pl.BlockSpec((1, A_DK), lambda b, h: (0, 0))],
        out_specs=pl.BlockSpec((1, t, A_DK), lambda b, h: (b, 0, h)),
        out_shape=jax.ShapeDtypeStruct((bsz, t, heads * A_DK), BF16),
        scratch_shapes=[pltpu.VMEM((t, A_DK), F32)] * 2,
        compiler_params=_params(("arbitrary", "arbitrary")),
        name="hgrn2",
    )(z, z, z, z, z, lb.reshape(1, heads * A_DK), norm_g.reshape(1, A_DK))


def _stacked_attention(q, k, v, bias, extra_ok):
    m_rows = q.shape[0]
    heads = q.shape[1] // ATT_DH
    head_of_lane = lax.broadcasted_iota(jnp.int32, (1, q.shape[1]), 1) // ATT_DH
    zero = jnp.zeros((), q.dtype)
    q_bd = jnp.concatenate([jnp.where(head_of_lane == h, q, zero) for h in range(heads)], axis=0)
    s = lax.dot_general(q_bd, k, (((1,), (1,)), ((), ())), preferred_element_type=F32) + bias
    if extra_ok is not None:
        s = jnp.where(extra_ok, s, MASK_VALUE)
    m = jnp.max(s, axis=-1, keepdims=True)
    p = jnp.exp(s - m)
    den = jnp.sum(p, axis=-1, keepdims=True)
    o = jnp.dot(p.astype(BF16), v, preferred_element_type=F32) / den
    lse = m + jnp.log(den)
    out = o[0:m_rows]
    lse_b = jnp.broadcast_to(lse[0:m_rows], out.shape)
    for h in range(1, heads):
        sel = head_of_lane == h
        out = jnp.where(sel, o[h * m_rows:(h + 1) * m_rows], out)
        lse_b = jnp.where(sel, lse[h * m_rows:(h + 1) * m_rows], lse_b)
    return out, lse_b


def _na_kernel(q_ref, k_ref, v_ref, bias_ref, o_ref, *, rows, kr):
    r = pl.program_id(1)
    rs = jnp.clip(r - kr // 2, 0, rows - kr)
    k0 = pl.multiple_of(rs * GRID_W, GRID_W)
    nk = kr * GRID_W
    scale = ATT_DH ** -0.5
    q = q_ref[0] * scale
    for s in range(q.shape[-1] // ATT_SLAB):
        sl = slice(s * ATT_SLAB, (s + 1) * ATT_SLAB)
        nb = ATT_HPS * GRID_W
        out, _ = _stacked_attention(q[:, sl], k_ref[0, pl.ds(k0, nk), sl], v_ref[0, pl.ds(k0, nk), sl],
                                    bias_ref[0, s * nb:(s + 1) * nb, :], None)
        o_ref[0, :, sl] = out.astype(o_ref.dtype)


def _na_bias_table(rpb, rows):
    heads = rpb.shape[0]
    kr = min(NA_KR, rows)
    col = np.arange(GRID_W)
    cs = np.clip(col - NA_KC // 2, 0, GRID_W - NA_KC)
    kc = np.arange(GRID_W)
    rel = kc[None, :] - col[:, None] + NA_KC - 1
    ok = (kc[None, :] >= cs[:, None]) & (kc[None, :] < cs[:, None] + NA_KC)
    rel_c = np.clip(rel, 0, 2 * NA_KC - 2)
    full = jnp.where(ok[None, None], rpb.astype(F32)[:, :, rel_c], MASK_VALUE)
    variants = []
    for var in range(kr):
        di0 = NA_KR - 1 - var
        blk = full[:, di0:di0 + kr]
        variants.append(jnp.transpose(blk, (0, 2, 1, 3)).reshape(heads * GRID_W, kr * GRID_W))
    return jnp.stack(variants)


def _neighbourhood_attention(zb, rpb):
    bsz, t, w3 = zb.shape
    width = w3 // 3
    heads = width // ATT_DH
    rows = t // GRID_W
    kr = min(NA_KR, rows)
    table = _na_bias_table(rpb, rows)

    def var_of(r):
        return r - jnp.clip(r - kr // 2, 0, rows - kr)

    return pl.pallas_call(
        functools.partial(_na_kernel, rows=rows, kr=kr),
        grid=(bsz, rows),
        in_specs=[
            pl.BlockSpec((1, GRID_W, width), lambda b, r: (b, r, 0)),
            pl.BlockSpec((1, t, width), lambda b, r: (b, 0, 1)),
            pl.BlockSpec((1, t, width), lambda b, r: (b, 0, 2)),
            pl.BlockSpec((1, heads * GRID_W, kr * GRID_W), lambda b, r: (var_of(r), 0, 0)),
        ],
        out_specs=pl.BlockSpec((1, GRID_W, width), lambda b, r: (b, r, 0)),
        out_shape=jax.ShapeDtypeStruct((bsz, t, width), BF16),
        compiler_params=_params(("arbitrary", "arbitrary")),
        name="neighbourhood_attention",
    )(zb, zb, zb, table)


def _dil_kernel(q_ref, kp_ref, k_ref, kn_ref, vp_ref, v_ref, vn_ref, bias_ref, o_ref, lse_ref,
                k_scr, v_scr, *, seq_len):
    i = pl.program_id(1)
    dil, cs = q_ref.shape[2], q_ref.shape[3]
    width = q_ref.shape[4]
    scale = ATT_DH ** -0.5
    nk = ATT_SUB + 2 * ATT_HALO
    nb = ATT_HPS * ATT_SUB
    kcol = lax.broadcasted_iota(jnp.int32, (1, nk), 1)
    for r in range(dil):
        k_scr[0:ATT_HALO] = kp_ref[0, 0, r]
        k_scr[ATT_HALO:ATT_HALO + cs] = k_ref[0, 0, r]
        k_scr[ATT_HALO + cs:] = kn_ref[0, 0, r]
        v_scr[0:ATT_HALO] = vp_ref[0, 0, r]
        v_scr[ATT_HALO:ATT_HALO + cs] = v_ref[0, 0, r]
        v_scr[ATT_HALO + cs:] = vn_ref[0, 0, r]
        for sb in range(cs // ATT_SUB):
            r0 = sb * ATT_SUB
            kpos = i * cs + r0 - ATT_HALO + kcol
            ok = (kpos >= 0) & (kpos < seq_len)
            q = q_ref[0, 0, r, r0:r0 + ATT_SUB, :] * scale
            if dil == 1:
                rows_out = pl.ds(r0, ATT_SUB)
            else:
                rows_out = pl.ds(r0 * dil + r, ATT_SUB, stride=dil)
            for s in range(width // ATT_SLAB):
                sl = slice(s * ATT_SLAB, (s + 1) * ATT_SLAB)
                out, lse = _stacked_attention(q[:, sl], k_scr[r0:r0 + nk, sl], v_scr[r0:r0 + nk, sl],
                                              bias_ref[s * nb:(s + 1) * nb, :], ok)
                for j in range(ATT_SLAB // LANES):
                    ch = s * (ATT_SLAB // LANES) + j
                    o_ref[0, ch, rows_out, :] = out[:, j * LANES:(j + 1) * LANES]
                    lse_ref[0, ch, rows_out, :] = lse[:, j * LANES:(j + 1) * LANES]


def _alibi_band(slopes_g, dil):
    rel = np.arange(ATT_SUB + 2 * ATT_HALO)[None, :] - ATT_HALO - np.arange(ATT_SUB)[:, None]
    band = np.abs(rel) <= ATT_HALO
    dist = jnp.asarray(np.abs(rel) * dil, F32)
    bias = jnp.where(band[None], -slopes_g[:, None, None] * dist[None], MASK_VALUE)
    return bias.reshape(slopes_g.shape[0] * ATT_SUB, ATT_SUB + 2 * ATT_HALO)


def _dilated_group(zp, dil, slopes_g, tm):
    bsz, t, f = zp.shape
    width = f // 3
    cs = tm // dil
    nt = t // tm
    zv = zp.reshape(bsz, nt, dil, cs, f)
    bias = _alibi_band(slopes_g, dil)
    tail = cs // ATT_HALO - 1

    def main(off):
        return pl.BlockSpec((1, 1, dil, cs, width), lambda b, i: (b, i, 0, 0, off))

    def prev(off):
        return pl.BlockSpec((1, 1, dil, ATT_HALO, width), lambda b, i: (b, jnp.maximum(i - 1, 0), 0, tail, off))

    def nxt(off):
        return pl.BlockSpec((1, 1, dil, ATT_HALO, width), lambda b, i: (b, jnp.minimum(i + 1, nt - 1), 0, 0, off))

    out_spec = pl.BlockSpec((1, width // LANES, tm, LANES), lambda b, i: (b, 0, i, 0))
    return pl.pallas_call(
        functools.partial(_dil_kernel, seq_len=t // dil),
        grid=(bsz, nt),
        in_specs=[main(0), prev(1), main(1), nxt(1), prev(2), main(2), nxt(2),
                  pl.BlockSpec(bias.shape, lambda b, i: (0, 0))],
        out_specs=[out_spec, out_spec],
        out_shape=[jax.ShapeDtypeStruct((bsz, width // LANES, t, LANES), F32)] * 2,
        scratch_shapes=[pltpu.VMEM((cs + 2 * ATT_HALO, width), BF16)] * 2,
        compiler_params=_params(("arbitrary", "arbitrary")),
        name=f"dilated_attention_d{dil}",
    )(zv, zv, zv, zv, zv, zv, zv, bias)


def _pool_kernel(up_ref, u_ref, un_ref, w_ref, sc_ref, o_ref, scr, *, seq_len):
    i = pl.program_id(1)
    tm = u_ref.shape[1]
    gc = w_ref.shape[1]
    scr[0:HALO_ROWS] = jnp.where(i > 0, up_ref[0], 0.0)
    scr[HALO_ROWS:HALO_ROWS + tm] = u_ref[0]
    scr[HALO_ROWS + tm:] = jnp.where(i < pl.num_programs(1) - 1, un_ref[0], 0.0)
    tpos = i * tm + lax.broadcasted_iota(jnp.int32, (tm, 1), 0)
    for g, w in enumerate(POOL_WINDOWS):
        sl = slice(g * gc, (g + 1) * gc)
        tot = jnp.zeros((tm, gc), F32)
        for off in range(-(w // 2), w - w // 2):
            tot = tot + scr[HALO_ROWS + off:HALO_ROWS + off + tm, sl]
        lo = jnp.clip(tpos - w // 2, 0, seq_len)
        hi = jnp.clip(tpos + w - w // 2, 0, seq_len)
        mean = tot / (hi - lo).astype(F32)
        diff = mean - scr[HALO_ROWS:HALO_ROWS + tm, sl]
        y = jnp.dot(diff.astype(BF16), w_ref[g], preferred_element_type=F32)
        o_ref[0, :, sl] = (y * sc_ref[:, sl]).astype(o_ref.dtype)


def _multiscale_pool(u, w_groups, scale, tm=512):
    bsz, t, ch = u.shape
    hb = tm // HALO_ROWS
    last = t // HALO_ROWS - 1
    return pl.pallas_call(
        functools.partial(_pool_kernel, seq_len=t),
        grid=(bsz, t // tm),
        in_specs=[
            pl.BlockSpec((1, HALO_ROWS, ch), lambda b, i: (b, jnp.maximum(i * hb - 1, 0), 0)),
            pl.BlockSpec((1, tm, ch), lambda b, i: (b, i, 0)),
            pl.BlockSpec((1, HALO_ROWS, ch), lambda b, i: (b, jnp.minimum((i + 1) * hb, last), 0)),
            pl.BlockSpec(w_groups.shape, lambda b, i: (0, 0, 0)),
            pl.BlockSpec((1, ch), lambda b, i: (0, 0)),
        ],
        out_specs=pl.BlockSpec((1, tm, ch), lambda b, i: (b, i, 0)),
        out_shape=jax.ShapeDtypeStruct((bsz, t, ch), BF16),
        scratch_shapes=[pltpu.VMEM((tm + 2 * HALO_ROWS, ch), F32)],
        compiler_params=_params(("arbitrary", "arbitrary")),
        name="multiscale_pool",
    )(u, u, u, w_groups.astype(BF16), scale.reshape(1, ch))


def _out_even_kernel(oa_ref, ob_ref, wa_ref, wb_ref, x_ref, mod_ref, o_ref, *, gate_row):
    mixed = jnp.dot(oa_ref[0], wa_ref[...], preferred_element_type=F32)
    mixed = mixed + jnp.dot(ob_ref[0], wb_ref[...], preferred_element_type=F32)
    gate = mod_ref[0][gate_row:gate_row + 1]
    o_ref[0] = x_ref[0] + gate * mixed


def _out_odd_kernel(o0_ref, o1_ref, o2_ref, l0_ref, l1_ref, l2_ref, od_ref, wa_ref, wb_ref, x_ref, mod_ref,
                    o_ref, *, gate_row):
    chunks = []
    for j in range(o0_ref.shape[1]):
        l0, l1, l2 = l0_ref[0, j], l1_ref[0, j], l2_ref[0, j]
        m = jnp.maximum(jnp.maximum(l0, l1), l2)
        e0, e1, e2 = jnp.exp(l0 - m), jnp.exp(l1 - m), jnp.exp(l2 - m)
        chunks.append((e0 * o0_ref[0, j] + e1 * o1_ref[0, j] + e2 * o2_ref[0, j]) / (e0 + e1 + e2))
    oc = jnp.concatenate(chunks, axis=1)
    mixed = jnp.dot(oc.astype(BF16), wa_ref[...], preferred_element_type=F32)
    mixed = mixed + jnp.dot(od_ref[0], wb_ref[...], preferred_element_type=F32)
    gate = mod_ref[0][gate_row:gate_row + 1]
    o_ref[0] = x_ref[0] + gate * mixed


def _out_proj(parts, w_out, x, mod, *, gate_row, odd, tm=512):
    bsz, t, d = x.shape
    half = w_out.shape[0] // 2
    wa, wb = w_out[:half].astype(BF16), w_out[half:].astype(BF16)
    tok = lambda width: pl.BlockSpec((1, tm, width), lambda b, i: (b, i, 0))
    chunked = pl.BlockSpec((1, half // LANES, tm, LANES), lambda b, i: (b, 0, i, 0))
    wspec = pl.BlockSpec((half, d), lambda b, i: (0, 0))
    kern = _out_odd_kernel if odd else _out_even_kernel
    part_specs = [chunked if p.ndim == 4 else tok(half) for p in parts]
    return pl.pallas_call(
        functools.partial(kern, gate_row=gate_row),
        grid=(bsz, t // tm),
        in_specs=part_specs + [wspec, wspec, tok(d), pl.BlockSpec((1, 6, d), lambda b, i: (b, 0, 0))],
        out_specs=tok(d),
        out_shape=jax.ShapeDtypeStruct((bsz, t, d), F32),
        compiler_params=_params(("arbitrary", "arbitrary")),
        name="out_proj_odd" if odd else "out_proj_even",
    )(*parts, wa, wb, x, mod)


def _ffn_up_kernel(xp_ref, x_ref, xn_ref, g_ref, mod_ref, w_ref, cw_ref, cb_ref, o_ref, h_scr, *, shift_row):
    i = pl.program_id(1)
    tm = x_ref.shape[1]
    dff = o_ref.shape[2]
    rows = tm + 2 * HALO_ROWS
    m = mod_ref[0]
    shift, scale = m[shift_row:shift_row + 1], m[shift_row + 1:shift_row + 2]
    g = g_ref[...]
    h_scr[0:HALO_ROWS] = _norm_mod(xp_ref[0], g, shift, scale).astype(BF16)
    h_scr[HALO_ROWS:HALO_ROWS + tm] = _norm_mod(x_ref[0], g, shift, scale).astype(BF16)
    h_scr[HALO_ROWS + tm:] = _norm_mod(xn_ref[0], g, shift, scale).astype(BF16)
    row = lax.broadcasted_iota(jnp.int32, (rows, 1), 0)
    inside = ((row >= HALO_ROWS) | (i > 0)) & ((row < HALO_ROWS + tm) | (i < pl.num_programs(1) - 1))
    keep = jnp.where(inside, 1.0, 0.0)
    h = h_scr[...]
    for c0 in range(0, dff, FFN_COLS):
        gate = jnp.dot(h, w_ref[:, c0:c0 + FFN_COLS], preferred_element_type=F32) * keep
        val = jnp.dot(h, w_ref[:, dff + c0:dff + c0 + FFN_COLS], preferred_element_type=F32)
        cw = cw_ref[:, c0:c0 + FFN_COLS]
        conv = cb_ref[:, c0:c0 + FFN_COLS] + gate * cw[1:2]
        conv = conv + pltpu.roll(gate, 1, 0) * cw[0:1]
        conv = conv + pltpu.roll(gate, rows - 1, 0) * cw[2:3]
        act = _silu(conv) * val
        o_ref[0, :, c0:c0 + FFN_COLS] = act[HALO_ROWS:HALO_ROWS + tm].astype(o_ref.dtype)


def _ffn_up(x, g, mod, w_up, conv_w, conv_b, *, shift_row, tm=1024):
    bsz, t, d = x.shape
    dff = w_up.shape[1] // 2
    hb = tm // HALO_ROWS
    last = t // HALO_ROWS - 1
    const = lambda shape: pl.BlockSpec(shape, lambda b, i: (0, 0))
    return pl.pallas_call(
        functools.partial(_ffn_up_kernel, shift_row=shift_row),
        grid=(bsz, t // tm),
        in_specs=[
            pl.BlockSpec((1, HALO_ROWS, d), lambda b, i: (b, jnp.maximum(i * hb - 1, 0), 0)),
            pl.BlockSpec((1, tm, d), lambda b, i: (b, i, 0)),
            pl.BlockSpec((1, HALO_ROWS, d), lambda b, i: (b, jnp.minimum((i + 1) * hb, last), 0)),
            const((1, d)),
            pl.BlockSpec((1, 6, d), lambda b, i: (b, 0, 0)),
            pl.BlockSpec((d, 2 * dff), lambda b, i: (0, 0), pipeline_mode=pl.Buffered(1)),
            const((FFN_CONV, dff)),
            const((1, dff)),
        ],
        out_specs=pl.BlockSpec((1, tm, dff), lambda b, i: (b, i, 0)),
        out_shape=jax.ShapeDtypeStruct((bsz, t, dff), BF16),
        scratch_shapes=[pltpu.VMEM((tm + 2 * HALO_ROWS, d), BF16)],
        compiler_params=_params(("arbitrary", "arbitrary")),
        name="ffn_up_conv",
    )(x, x, x, g.reshape(1, d), mod, w_up.astype(BF16), conv_w, conv_b.reshape(1, dff))


def _ffn_down_kernel(a_ref, w_ref, x_ref, mod_ref, o_ref, *, gate_row):
    y = jnp.dot(a_ref[0], w_ref[...], preferred_element_type=F32)
    o_ref[0] = x_ref[0] + mod_ref[0][gate_row:gate_row + 1] * y


def _ffn_down(a, w_down, x, mod, *, gate_row, tm=512):
    bsz, t, d = x.shape
    dff = a.shape[-1]
    return pl.pallas_call(
        functools.partial(_ffn_down_kernel, gate_row=gate_row),
        grid=(bsz, t // tm),
        in_specs=[
            pl.BlockSpec((1, tm, dff), lambda b, i: (b, i, 0)),
            pl.BlockSpec((dff, d), lambda b, i: (0, 0)),
            pl.BlockSpec((1, tm, d), lambda b, i: (b, i, 0)),
            pl.BlockSpec((1, 6, d), lambda b, i: (b, 0, 0)),
        ],
        out_specs=pl.BlockSpec((1, tm, d), lambda b, i: (b, i, 0)),
        out_shape=jax.ShapeDtypeStruct((bsz, t, d), F32),
        compiler_params=_params(("arbitrary", "arbitrary")),
        name="ffn_down",
    )(a, w_down.astype(BF16), x, mod)


def _final_norm_kernel(x_ref, g_ref, o_ref):
    x = x_ref[0]
    o_ref[0] = x * lax.rsqrt(jnp.mean(x * x, axis=-1, keepdims=True) + RMS_EPS) * g_ref[...]


def _final_norm(x, g, tm=1024):
    bsz, t, d = x.shape
    return pl.pallas_call(
        _final_norm_kernel,
        grid=(bsz, t // tm),
        in_specs=[pl.BlockSpec((1, tm, d), lambda b, i: (b, i, 0)), pl.BlockSpec((1, d), lambda b, i: (0, 0))],
        out_specs=pl.BlockSpec((1, tm, d), lambda b, i: (b, i, 0)),
        out_shape=jax.ShapeDtypeStruct((bsz, t, d), F32),
        compiler_params=_params(("arbitrary", "arbitrary")),
        name="final_norm",
    )(x, g.reshape(1, d))


def kernel(x, c, ada_w, ada_b, norm_mix_g, norm_ffn_g, even_w_in, even_w_out, hgrn_lb_logits, hgrn_norm_g, na_rpb, odd_w_in, odd_w_out, pool_w, pool_scale, ffn_w_up, ffn_conv_w, ffn_conv_b, ffn_w_down, final_norm_g):
    bsz, t, d = x.shape
    depth = ada_w.shape[0]
    a_width = hgrn_lb_logits.shape[1]
    n_groups = len(C_CONFIGS)
    c_width = C_HPG * ATT_DH
    n_c = 3 * n_groups * c_width

    lb_soft = jax.nn.softmax(hgrn_lb_logits.astype(F32), axis=0)
    lower_bounds = jnp.cumsum(lb_soft, axis=0) - lb_soft[0]
    slopes = jnp.exp2(-8.0 * jnp.arange(1, n_groups * C_HPG + 1, dtype=F32) / (n_groups * C_HPG))
    slopes = slopes.reshape(n_groups, C_HPG)

    mod_all = _ada_mod(c, ada_w, ada_b).reshape(depth, bsz, 6, d)

    for l in range(depth):
        mod = mod_all[l]
        if l % 2 == 0:
            e = l // 2
            w_in = even_w_in[e].astype(BF16)
            z_a = _norm_mod_matmul(x, norm_mix_g[l], mod, w_in[:, :5 * a_width], shift_row=0, out_dtype=F32, tm=512)
            z_b = _norm_mod_matmul(x, norm_mix_g[l], mod, w_in[:, 5 * a_width:], shift_row=0, out_dtype=BF16)
            o_a = _hgrn2(z_a, lower_bounds[e], hgrn_norm_g[e])
            o_b = _neighbourhood_attention(z_b, na_rpb[e])
            x = _out_proj([o_a, o_b], even_w_out[e], x, mod, gate_row=2, odd=False)
        else:
            o_i = l // 2
            w_in = odd_w_in[o_i].astype(BF16)
            z_d = _norm_mod_matmul(x, norm_mix_g[l], mod, w_in[:, n_c:], shift_row=0, out_dtype=F32)
            outs, lses = [], []
            for g, (_, dil) in enumerate(C_CONFIGS):
                w_g = jnp.concatenate([w_in[:, (k * n_groups + g) * c_width:(k * n_groups + g + 1) * c_width]
                                       for k in range(3)], axis=1)
                z_g = _norm_mod_matmul(x, norm_mix_g[l], mod, w_g, shift_row=0, out_dtype=BF16,
                                       tm=DIL_TILE[dil], perm=dil)
                o_g, lse_g = _dilated_group(z_g, dil, slopes[g], DIL_TILE[dil])
                outs.append(o_g)
                lses.append(lse_g)
            o_d = _multiscale_pool(z_d, pool_w[o_i], pool_scale[o_i])
            x = _out_proj(outs + lses + [o_d], odd_w_out[o_i], x, mod, gate_row=2, odd=True)
        a = _ffn_up(x, norm_ffn_g[l], mod, ffn_w_up[l], ffn_conv_w[l], ffn_conv_b[l], shift_row=3)
        x = _ffn_down(a, ffn_w_down[l], x, mod, gate_row=5)
    return _final_norm(x, final_norm_g)
```

```python
import functools

import jax
import jax.numpy as jnp
import numpy as np
from jax import lax
from jax.experimental import pallas as pl
from jax.experimental.pallas import tpu as pltpu

F32 = jnp.float32
BF16 = jnp.bfloat16

GRID_W = 64
RMS_EPS = 1e-6
LB_FLOOR = 1e-12
MASK_VALUE = -1e30
A_DK = 128
HGRN_TILE = 128
HGRN_GROUP = 8
HGRN_LEVELS = (8, 16, 32, 64)
ATT_DH = 64
ATT_SLAB = 256
ATT_HPS = ATT_SLAB // ATT_DH
NA_KR = 8
NA_KC = 16
NA_ROWS_PER_STEP = 8
C_CONFIGS = ((128, 1), (512, 4), (2048, 16))
C_HPG = 8
DIL_TILE = {1: 1024, 4: 1024, 16: 2048}
ATT_SUB = 128
ATT_HALO = 64
POOL_WINDOWS = (2, 4, 8, 16)
POOL_HALO = 8
FFN_CONV = 3
FFN_COLS = 256
HALO_ROWS = 8
LANES = 128
VMEM_LIMIT = 56 * 1024 * 1024


def _params(sem):
    return pltpu.CompilerParams(dimension_semantics=sem, vmem_limit_bytes=VMEM_LIMIT)


def _silu(v):
    return v * (1.0 / (1.0 + jnp.exp(-v)))


def _norm_mod(x, g, shift, scale):
    ms = jnp.mean(x * x, axis=-1, keepdims=True)
    return x * lax.rsqrt(ms + RMS_EPS) * (g * (1.0 + scale)) + shift


def _ada_kernel(c_ref, w_ref, b_ref, o_ref):
    c = c_ref[...]
    ca = _silu(c).astype(BF16)
    o_ref[0] = jnp.dot(ca, w_ref[0].astype(BF16), preferred_element_type=F32) + b_ref[0]


def _ada_mod(c, ada_w, ada_b):
    depth, d, n = ada_w.shape
    bsz = c.shape[0]
    tn = 1024
    return pl.pallas_call(
        _ada_kernel,
        grid=(depth, n // tn),
        in_specs=[
            pl.BlockSpec((bsz, d), lambda l, j: (0, 0)),
            pl.BlockSpec((1, d, tn), lambda l, j: (l, 0, j)),
            pl.BlockSpec((1, 1, tn), lambda l, j: (l, 0, j)),
        ],
        out_specs=pl.BlockSpec((1, bsz, tn), lambda l, j: (l, 0, j)),
        out_shape=jax.ShapeDtypeStruct((depth, bsz, n), F32),
        compiler_params=_params(("arbitrary", "arbitrary")),
        name="ada_mod",
    )(c, ada_w, ada_b.reshape(depth, 1, n))


def _in_proj_kernel(x_ref, g_ref, mod_ref, *rest, shift_row, perm, col_groups):
    n_out = len(col_groups)
    n_w = sum(len(ranges) for ranges in col_groups)
    w_refs, rest = rest[:n_w], rest[n_w:]
    o_refs, h_ref = rest[:n_out], rest[n_out]
    m = mod_ref[0]
    h = _norm_mod(x_ref[0], g_ref[...], m[shift_row:shift_row + 1], m[shift_row + 1:shift_row + 2])
    if perm == 1:
        h_ref[...] = h.astype(BF16)
    else:
        h32 = rest[n_out + 1]
        cs = h.shape[0] // perm
        for j in range(h32.shape[0]):
            h32[j] = h[:, j * LANES:(j + 1) * LANES]
        for r in range(perm):
            for j in range(h32.shape[0]):
                h_ref[r * cs:(r + 1) * cs, j * LANES:(j + 1) * LANES] = (
                    h32[j, pl.ds(r, cs, stride=perm), :].astype(BF16))
    hb = h_ref[...]
    w_iter = iter(w_refs)
    for o_ref, ranges in zip(o_refs, col_groups):
        off = 0
        for _, width in ranges:
            o_ref[0, :, off:off + width] = jnp.dot(hb, next(w_iter)[...],
                                                   preferred_element_type=F32).astype(o_ref.dtype)
            off += width


def _in_proj(x, g_all, mod_all, w_all, layer, w_layer, col_groups, out_dtypes, *, tm=1024, perm=1):
    bsz, t, d = x.shape
    widths = [sum(w for _, w in ranges) for ranges in col_groups]
    w_specs = []
    for ranges in col_groups:
        for start, width in ranges:
            assert start % width == 0
            w_specs.append(pl.BlockSpec((None, d, width), lambda b, i, blk=start // width: (w_layer, 0, blk),
                                        pipeline_mode=pl.Buffered(1)))
    return pl.pallas_call(
        functools.partial(_in_proj_kernel, shift_row=0, perm=perm, col_groups=col_groups),
        grid=(bsz, t // tm),
        in_specs=[
            pl.BlockSpec((1, tm, d), lambda b, i: (b, i, 0)),
            pl.BlockSpec((None, 1, d), lambda b, i: (layer, 0, 0)),
            pl.BlockSpec((None, 1, 6, d), lambda b, i: (layer, b, 0, 0)),
        ] + w_specs,
        out_specs=[pl.BlockSpec((1, tm, w), lambda b, i: (b, i, 0)) for w in widths],
        out_shape=[jax.ShapeDtypeStruct((bsz, t, w), dt) for w, dt in zip(widths, out_dtypes)],
        scratch_shapes=[pltpu.VMEM((tm, d), BF16)] + ([pltpu.VMEM((d // LANES, tm, LANES), F32)] if perm > 1 else []),
        compiler_params=_params(("arbitrary", "arbitrary")),
        name="in_proj",
    )(x, g_all, mod_all, *([w_all] * len(w_specs)))


def _split3(x):
    hi = x.astype(BF16)
    r = x - hi.astype(F32)
    mid = r.astype(BF16)
    lo = (r - mid.astype(F32)).astype(BF16)
    return hi, mid, lo


def _span_code(later, earlier):
    g = HGRN_GROUP
    code = jnp.where(((later // g) == (earlier // g)) & (later >= earlier), later - earlier, -1)
    for li, m in enumerate(HGRN_LEVELS):
        straddles = ((later // (2 * m)) == (earlier // (2 * m))) & ((later % (2 * m)) >= m) & ((earlier % (2 * m)) < m)
        code = jnp.where(straddles, g + li, code)
    return code


def _hgrn_kernel(q_ref, ff_ref, fb_ref, v_ref, g_ref, lb_ref, ng_ref, o_ref, acc_scr, qs_scr):
    t = q_ref.shape[1]
    tl = HGRN_TILE
    g8 = HGRN_GROUP
    n_tiles = t // tl
    log2e = 1.4426950408889634

    lb = lb_ref[...]
    lb_floor = jnp.maximum(lb, LB_FLOOR)
    one_m_lb = 1.0 - lb
    log_one_m_lb = jnp.log(one_m_lb)

    row = lax.broadcasted_iota(jnp.int32, (tl, tl), 0)
    col = lax.broadcasted_iota(jnp.int32, (tl, tl), 1)
    tri = jnp.where(col <= row, 1.0, 0.0).astype(BF16)
    code_f = _span_code(row, col)
    code_b = _span_code(col, row)
    r2 = lax.broadcasted_iota(jnp.int32, (2 * tl, 2 * tl), 0) < tl
    c2 = lax.broadcasted_iota(jnp.int32, (2 * tl, 2 * tl), 1) < tl
    ones_bd = jnp.where(r2 == c2, 1.0, 0.0).astype(BF16)

    def gates(fz):
        e = jnp.exp(-jnp.abs(fz))
        r = 1.0 / (1.0 + e)
        pos = fz >= 0
        sig_pos = jnp.where(pos, r, e * r)
        sig_neg = jnp.where(pos, e * r, r)
        log_sig_neg = -(jnp.maximum(fz, 0.0) + jnp.log(1.0 + e))
        return jnp.log(lb_floor + one_m_lb * sig_pos), one_m_lb * sig_neg, log_one_m_lb + log_sig_neg

    def rot_group(x, shift):
        return pltpu.roll(x.reshape(tl // g8, g8, x.shape[1]), shift % g8, 1).reshape(x.shape)

    def span_ref(b, fwd, m):
        blocks = []
        for a in range(0, tl, 2 * m):
            edge = a + m - 1 if fwd else a + m
            blocks.append(jnp.broadcast_to(b[edge:edge + 1], (2 * m, b.shape[1])))
        return jnp.concatenate(blocks, axis=0)

    acc_scr[...] = jnp.zeros_like(acc_scr)

    def silu_body(i, carry):
        t0 = pl.multiple_of(i * tl, tl)
        qs_scr[pl.ds(t0, tl), :] = _silu(q_ref[0, pl.ds(t0, tl), :])
        return carry

    lax.fori_loop(0, n_tiles, silu_body, 0)

    def body(n, carry):
        st_f, st_b = carry
        tf = pl.multiple_of(n * tl, tl)
        tb = pl.multiple_of((n_tiles - 1 - n) * tl, tl)
        q_f = qs_scr[pl.ds(tf, tl), :]
        q_b = qs_scr[pl.ds(tb, tl), :]
        v_f = v_ref[0, pl.ds(tf, tl), :].astype(BF16)
        v_b = v_ref[0, pl.ds(tb, tl), :].astype(BF16)
        lf_f, kk_f, lk_f = gates(ff_ref[0, pl.ds(tf, tl), :])
        lf_b, kk_b, lk_b = gates(fb_ref[0, pl.ds(tb, tl), :])

        pieces = jnp.concatenate(_split3(lf_f) + _split3(lf_b), axis=1)
        ps = jnp.dot(tri, pieces, preferred_element_type=F32)
        b_f = ps[:, 0:A_DK] + ps[:, A_DK:2 * A_DK] + ps[:, 2 * A_DK:3 * A_DK]
        p_b = ps[:, 3 * A_DK:4 * A_DK] + ps[:, 4 * A_DK:5 * A_DK] + ps[:, 5 * A_DK:6 * A_DK]
        b_b = p_b[tl - 1:tl] - p_b + lf_b

        b2_f, w2_f = b_f * log2e, (lk_f - b_f) * log2e
        b2_b, w2_b = b_b * log2e, (lk_b - b_b) * log2e
        lhs = []
        for e in range(g8):
            wf_e = w2_f if e == 0 else rot_group(w2_f, e)
            wb_e = w2_b if e == 0 else rot_group(w2_b, g8 - e)
            pf = q_f * jnp.exp2(jnp.minimum(b2_f + wf_e, 0.0))
            pb = q_b * jnp.exp2(jnp.minimum(b2_b + wb_e, 0.0))
            lhs.append(jnp.concatenate([pf, pb], axis=1).astype(BF16))
        half = g8 // 2
        att = [jnp.dot(jnp.concatenate(part, axis=0), ones_bd, preferred_element_type=F32)
               for part in (lhs[:half], lhs[half:])]
        a_f = jnp.zeros((tl, tl), F32)
        a_b = jnp.zeros((tl, tl), F32)
        for e in range(g8):
            blk = att[e // half][(e % half) * tl:(e % half + 1) * tl]
            a_f = jnp.where(code_f == e, blk[:, :A_DK], a_f)
            a_b = jnp.where(code_b == e, blk[:, A_DK:], a_b)

        for li, m in enumerate(HGRN_LEVELS):
            e_f = jnp.exp(-jnp.abs(b_f - span_ref(b_f, True, m)))
            e_b = jnp.exp(-jnp.abs(b_b - span_ref(b_b, False, m)))
            s_f = lax.dot_general((q_f * e_f).astype(BF16), (kk_f * e_f).astype(BF16),
                                  (((1,), (1,)), ((), ())), preferred_element_type=F32)
            s_b = lax.dot_general((q_b * e_b).astype(BF16), (kk_b * e_b).astype(BF16),
                                  (((1,), (1,)), ((), ())), preferred_element_type=F32)
            a_f = jnp.where(code_f == g8 + li, s_f, a_f)
            a_b = jnp.where(code_b == g8 + li, s_b, a_b)

        def inter(q, v_bf, kk, b, edge, st):
            b_edge = b[edge:edge + 1]
            qb = (q * jnp.exp(b)).astype(BF16)
            o = lax.dot_general(qb, st.astype(BF16), (((1,), (1,)), ((), ())), preferred_element_type=F32)
            kd = (kk * jnp.exp(b_edge - b)).astype(BF16)
            kv = lax.dot_general(v_bf, kd, (((0,), (0,)), ((), ())), preferred_element_type=F32)
            return o, st * jnp.exp(b_edge) + kv

        oi_f, st_f = inter(q_f, v_f, kk_f, b_f, tl - 1, st_f)
        oi_b, st_b = inter(q_b, v_b, kk_b, b_b, 0, st_b)
        acc_scr[pl.ds(tf, tl), :] += jnp.dot(a_f.astype(BF16), v_f, preferred_element_type=F32) + oi_f
        acc_scr[pl.ds(tb, tl), :] += jnp.dot(a_b.astype(BF16), v_b, preferred_element_type=F32) + oi_b
        return st_f, st_b

    zero = jnp.zeros((A_DK, A_DK), F32)
    lax.fori_loop(0, n_tiles, body, (zero, zero))

    ng = ng_ref[...]

    def out_body(i, carry):
        t0 = pl.multiple_of(i * tl, tl)
        o = acc_scr[pl.ds(t0, tl), :]
        o = o * lax.rsqrt(jnp.mean(o * o, axis=-1, keepdims=True) + RMS_EPS) * ng
        o_ref[0, pl.ds(t0, tl), :] = (o * _silu(g_ref[0, pl.ds(t0, tl), :])).astype(o_ref.dtype)
        return carry

    lax.fori_loop(0, n_tiles, out_body, 0)


def _hgrn2(z, lb, norm_g):
    bsz, t, w5 = z.shape
    heads = w5 // 5 // A_DK

    def col(k):
        return pl.BlockSpec((1, t, A_DK), lambda b, h, k=k: (b, 0, h + heads * k))

    return pl.pallas_call(
        _hgrn_kernel,
        grid=(bsz, heads),
        in_specs=[col(0), col(1), col(2), col(3), col(4),
                  pl.BlockSpec((1, A_DK), lambda b, h: (0, h)),
                  pl.BlockSpec((1, A_DK), lambda b, h: (0, 0))],
        out_specs=pl.BlockSpec((1, t, A_DK), lambda b, h: (b, 0, h)),
        out_shape=jax.ShapeDtypeStruct((bsz, t, heads * A_DK), BF16),
        scratch_shapes=[pltpu.VMEM((t, A_DK), F32)] * 2,
        compiler_params=_params(("arbitrary", "arbitrary")),
        name="hgrn2",
    )(z, z, z, z, z, lb.reshape(1, heads * A_DK), norm_g.reshape(1, A_DK))


def _stacked_attention(q, k, v, bias, extra_ok):
    m_rows = q.shape[0]
    heads = q.shape[1] // ATT_DH
    head_of_lane = lax.broadcasted_iota(jnp.int32, (1, q.shape[1]), 1) // ATT_DH
    zero = jnp.zeros((), q.dtype)
    q_bd = jnp.concatenate([jnp.where(head_of_lane == h, q, zero) for h in range(heads)], axis=0)
    s = lax.dot_general(q_bd, k, (((1,), (1,)), ((), ())), preferred_element_type=F32) + bias
    if extra_ok is not None:
        s = jnp.where(extra_ok, s, MASK_VALUE)
    m = jnp.max(s, axis=-1, keepdims=True)
    p = jnp.exp(s - m)
    den = jnp.sum(p, axis=-1, keepdims=True)
    o = jnp.dot(p.astype(BF16), v, preferred_element_type=F32) / den
    lse = m + jnp.log(den)
    out = o[0:m_rows]
    lse_b = jnp.broadcast_to(lse[0:m_rows], out.shape)
    for h in range(1, heads):
        sel = head_of_lane == h
        out = jnp.where(sel, o[h * m_rows:(h + 1) * m_rows], out)
        lse_b = jnp.where(sel, lse[h * m_rows:(h + 1) * m_rows], lse_b)
    return out, lse_b


def _na_kernel(q_ref, k_ref, v_ref, *rest, rows, kr):
    bias_refs, o_ref = rest[:-1], rest[-1]
    nk = kr * GRID_W
    nb = ATT_HPS * GRID_W
    scale = ATT_DH ** -0.5
    for j, bias_ref in enumerate(bias_refs):
        r = pl.program_id(1) * len(bias_refs) + j
        rs = jnp.clip(r - kr // 2, 0, rows - kr)
        k0 = pl.multiple_of(rs * GRID_W, GRID_W)
        q = q_ref[0, j * GRID_W:(j + 1) * GRID_W, :] * scale
        for s in range(q.shape[-1] // ATT_SLAB):
            sl = slice(s * ATT_SLAB, (s + 1) * ATT_SLAB)
            out, _ = _stacked_attention(q[:, sl], k_ref[0, pl.ds(k0, nk), sl], v_ref[0, pl.ds(k0, nk), sl],
                                        bias_ref[0, s * nb:(s + 1) * nb, :], None)
            o_ref[0, j * GRID_W:(j + 1) * GRID_W, sl] = out.astype(o_ref.dtype)


def _na_bias_table(rpb, rows):
    heads = rpb.shape[0]
    kr = min(NA_KR, rows)
    col = np.arange(GRID_W)
    cs = np.clip(col - NA_KC // 2, 0, GRID_W - NA_KC)
    kc = np.arange(GRID_W)
    rel = kc[None, :] - col[:, None] + NA_KC - 1
    ok = (kc[None, :] >= cs[:, None]) & (kc[None, :] < cs[:, None] + NA_KC)
    rel_c = np.clip(rel, 0, 2 * NA_KC - 2)
    full = jnp.where(ok[None, None], rpb.astype(F32)[:, :, rel_c], MASK_VALUE)
    variants = []
    for var in range(kr):
        di0 = NA_KR - 1 - var
        blk = full[:, di0:di0 + kr]
        variants.append(jnp.transpose(blk, (0, 2, 1, 3)).reshape(heads * GRID_W, kr * GRID_W))
    return jnp.stack(variants)


def _neighbourhood_attention(zb, rpb):
    bsz, t, w3 = zb.shape
    width = w3 // 3
    heads = width // ATT_DH
    rows = t // GRID_W
    kr = min(NA_KR, rows)
    table = _na_bias_table(rpb, rows)

    rps = NA_ROWS_PER_STEP

    def bias_spec(j):
        def index(b, i):
            r = i * rps + j
            return (r - jnp.clip(r - kr // 2, 0, rows - kr), 0, 0)
        return pl.BlockSpec((1, heads * GRID_W, kr * GRID_W), index)

    return pl.pallas_call(
        functools.partial(_na_kernel, rows=rows, kr=kr),
        grid=(bsz, rows // rps),
        in_specs=[
            pl.BlockSpec((1, rps * GRID_W, width), lambda b, i: (b, i, 0)),
            pl.BlockSpec((1, t, width), lambda b, i: (b, 0, 1)),
            pl.BlockSpec((1, t, width), lambda b, i: (b, 0, 2)),
        ] + [bias_spec(j) for j in range(rps)],
        out_specs=pl.BlockSpec((1, rps * GRID_W, width), lambda b, i: (b, i, 0)),
        out_shape=jax.ShapeDtypeStruct((bsz, t, width), BF16),
        compiler_params=_params(("arbitrary", "arbitrary")),
        name="neighbourhood_attention",
    )(zb, zb, zb, *([table] * rps))


def _dil_kernel(q_ref, kp_ref, k_ref, kn_ref, vp_ref, v_ref, vn_ref, bias_ref, o_ref, lse_ref,
                k_scr, v_scr, *, seq_len):
    i = pl.program_id(1)
    dil, cs = q_ref.shape[2], q_ref.shape[3]
    width = q_ref.shape[4]
    scale = ATT_DH ** -0.5
    nk = ATT_SUB + 2 * ATT_HALO
    nb = ATT_HPS * ATT_SUB
    kcol = lax.broadcasted_iota(jnp.int32, (1, nk), 1)
    for r in range(dil):
        k_scr[0:ATT_HALO] = kp_ref[0, 0, r]
        k_scr[ATT_HALO:ATT_HALO + cs] = k_ref[0, 0, r]
        k_scr[ATT_HALO + cs:] = kn_ref[0, 0, r]
        v_scr[0:ATT_HALO] = vp_ref[0, 0, r]
        v_scr[ATT_HALO:ATT_HALO + cs] = v_ref[0, 0, r]
        v_scr[ATT_HALO + cs:] = vn_ref[0, 0, r]
        for sb in range(cs // ATT_SUB):
            r0 = sb * ATT_SUB
            kpos = i * cs + r0 - ATT_HALO + kcol
            ok = (kpos >= 0) & (kpos < seq_len)
            q = q_ref[0, 0, r, r0:r0 + ATT_SUB, :] * scale
            if dil == 1:
                rows_out = pl.ds(r0, ATT_SUB)
            else:
                rows_out = pl.ds(r0 * dil + r, ATT_SUB, stride=dil)
            for s in range(width // ATT_SLAB):
                sl = slice(s * ATT_SLAB, (s + 1) * ATT_SLAB)
                out, lse = _stacked_attention(q[:, sl], k_scr[r0:r0 + nk, sl], v_scr[r0:r0 + nk, sl],
                                              bias_ref[s * nb:(s + 1) * nb, :], ok)
                for j in range(ATT_SLAB // LANES):
                    ch = s * (ATT_SLAB // LANES) + j
                    o_ref[0, ch, rows_out, :] = out[:, j * LANES:(j + 1) * LANES]
                    lse_ref[0, ch, rows_out, :] = lse[:, j * LANES:(j + 1) * LANES]


def _alibi_band(slopes_g, dil):
    rel = np.arange(ATT_SUB + 2 * ATT_HALO)[None, :] - ATT_HALO - np.arange(ATT_SUB)[:, None]
    band = np.abs(rel) <= ATT_HALO
    dist = jnp.asarray(np.abs(rel) * dil, F32)
    bias = jnp.where(band[None], -slopes_g[:, None, None] * dist[None], MASK_VALUE)
    return bias.reshape(slopes_g.shape[0] * ATT_SUB, ATT_SUB + 2 * ATT_HALO)


def _dilated_group(zp, dil, slopes_g, tm):
    bsz, t, f = zp.shape
    width = f // 3
    cs = tm // dil
    nt = t // tm
    zv = zp.reshape(bsz, nt, dil, cs, f)
    bias = _alibi_band(slopes_g, dil)
    tail = cs // ATT_HALO - 1

    def main(off):
        return pl.BlockSpec((1, 1, dil, cs, width), lambda b, i: (b, i, 0, 0, off))

    def prev(off):
        return pl.BlockSpec((1, 1, dil, ATT_HALO, width), lambda b, i: (b, jnp.maximum(i - 1, 0), 0, tail, off))

    def nxt(off):
        return pl.BlockSpec((1, 1, dil, ATT_HALO, width), lambda b, i: (b, jnp.minimum(i + 1, nt - 1), 0, 0, off))

    out_spec = pl.BlockSpec((1, width // LANES, tm, LANES), lambda b, i: (b, 0, i, 0))
    return pl.pallas_call(
        functools.partial(_dil_kernel, seq_len=t // dil),
        grid=(bsz, nt),
        in_specs=[main(0), prev(1), main(1), nxt(1), prev(2), main(2), nxt(2),
                  pl.BlockSpec(bias.shape, lambda b, i: (0, 0))],
        out_specs=[out_spec, out_spec],
        out_shape=[jax.ShapeDtypeStruct((bsz, width // LANES, t, LANES), F32)] * 2,
        scratch_shapes=[pltpu.VMEM((cs + 2 * ATT_HALO, width), BF16)] * 2,
        compiler_params=_params(("arbitrary", "arbitrary")),
        name=f"dilated_attention_d{dil}",
    )(zv, zv, zv, zv, zv, zv, zv, bias)


def _pool_kernel(up_ref, u_ref, un_ref, w_ref, sc_ref, o_ref, scr, *, seq_len):
    i = pl.program_id(1)
    tm = u_ref.shape[1]
    gc = w_ref.shape[1]
    scr[0:HALO_ROWS] = jnp.where(i > 0, up_ref[0], 0.0)
    scr[HALO_ROWS:HALO_ROWS + tm] = u_ref[0]
    scr[HALO_ROWS + tm:] = jnp.where(i < pl.num_programs(1) - 1, un_ref[0], 0.0)
    tpos = i * tm + lax.broadcasted_iota(jnp.int32, (tm, 1), 0)
    for g, w in enumerate(POOL_WINDOWS):
        sl = slice(g * gc, (g + 1) * gc)
        tot = jnp.zeros((tm, gc), F32)
        for off in range(-(w // 2), w - w // 2):
            tot = tot + scr[HALO_ROWS + off:HALO_ROWS + off + tm, sl]
        lo = jnp.clip(tpos - w // 2, 0, seq_len)
        hi = jnp.clip(tpos + w - w // 2, 0, seq_len)
        mean = tot / (hi - lo).astype(F32)
        diff = mean - scr[HALO_ROWS:HALO_ROWS + tm, sl]
        y = jnp.dot(diff.astype(BF16), w_ref[g], preferred_element_type=F32)
        o_ref[0, :, sl] = (y * sc_ref[:, sl]).astype(o_ref.dtype)


def _multiscale_pool(u, w_groups, scale, tm=512):
    bsz, t, ch = u.shape
    hb = tm // HALO_ROWS
    last = t // HALO_ROWS - 1
    return pl.pallas_call(
        functools.partial(_pool_kernel, seq_len=t),
        grid=(bsz, t // tm),
        in_specs=[
            pl.BlockSpec((1, HALO_ROWS, ch), lambda b, i: (b, jnp.maximum(i * hb - 1, 0), 0)),
            pl.BlockSpec((1, tm, ch), lambda b, i: (b, i, 0)),
            pl.BlockSpec((1, HALO_ROWS, ch), lambda b, i: (b, jnp.minimum((i + 1) * hb, last), 0)),
            pl.BlockSpec(w_groups.shape, lambda b, i: (0, 0, 0)),
            pl.BlockSpec((1, ch), lambda b, i: (0, 0)),
        ],
        out_specs=pl.BlockSpec((1, tm, ch), lambda b, i: (b, i, 0)),
        out_shape=jax.ShapeDtypeStruct((bsz, t, ch), BF16),
        scratch_shapes=[pltpu.VMEM((tm + 2 * HALO_ROWS, ch), F32)],
        compiler_params=_params(("arbitrary", "arbitrary")),
        name="multiscale_pool",
    )(u, u, u, w_groups.astype(BF16), scale.reshape(1, ch))


def _out_even_kernel(oa_ref, ob_ref, wa_ref, wb_ref, x_ref, mod_ref, o_ref, *, gate_row):
    mixed = jnp.dot(oa_ref[0], wa_ref[...], preferred_element_type=F32)
    mixed = mixed + jnp.dot(ob_ref[0], wb_ref[...], preferred_element_type=F32)
    gate = mod_ref[0][gate_row:gate_row + 1]
    o_ref[0] = x_ref[0] + gate * mixed


def _out_odd_kernel(o0_ref, o1_ref, o2_ref, l0_ref, l1_ref, l2_ref, od_ref, wa_ref, wb_ref, x_ref, mod_ref,
                    o_ref, *, gate_row):
    chunks = []
    for j in range(o0_ref.shape[1]):
        l0, l1, l2 = l0_ref[0, j], l1_ref[0, j], l2_ref[0, j]
        m = jnp.maximum(jnp.maximum(l0, l1), l2)
        e0, e1, e2 = jnp.exp(l0 - m), jnp.exp(l1 - m), jnp.exp(l2 - m)
        chunks.append((e0 * o0_ref[0, j] + e1 * o1_ref[0, j] + e2 * o2_ref[0, j]) / (e0 + e1 + e2))
    oc = jnp.concatenate(chunks, axis=1)
    mixed = jnp.dot(oc.astype(BF16), wa_ref[...], preferred_element_type=F32)
    mixed = mixed + jnp.dot(od_ref[0], wb_ref[...], preferred_element_type=F32)
    gate = mod_ref[0][gate_row:gate_row + 1]
    o_ref[0] = x_ref[0] + gate * mixed


def _out_proj(parts, w_all, w_layer, x, mod_all, layer, *, gate_row, odd, tm=512):
    bsz, t, d = x.shape
    half = w_all.shape[1] // 2
    tok = lambda width: pl.BlockSpec((1, tm, width), lambda b, i: (b, i, 0))
    chunked = pl.BlockSpec((1, half // LANES, tm, LANES), lambda b, i: (b, 0, i, 0))
    kern = _out_odd_kernel if odd else _out_even_kernel
    part_specs = [chunked if p.ndim == 4 else tok(half) for p in parts]
    return pl.pallas_call(
        functools.partial(kern, gate_row=gate_row),
        grid=(bsz, t // tm),
        in_specs=part_specs + [
            pl.BlockSpec((None, half, d), lambda b, i: (w_layer, 0, 0)),
            pl.BlockSpec((None, half, d), lambda b, i: (w_layer, 1, 0)),
            tok(d),
            pl.BlockSpec((None, 1, 6, d), lambda b, i: (layer, b, 0, 0)),
        ],
        out_specs=tok(d),
        out_shape=jax.ShapeDtypeStruct((bsz, t, d), F32),
        compiler_params=_params(("arbitrary", "arbitrary")),
        name="out_proj_odd" if odd else "out_proj_even",
    )(*parts, w_all, w_all, x, mod_all)


def _ffn_up_kernel(xp_ref, x_ref, xn_ref, g_ref, mod_ref, w_ref, cw_ref, cb_ref, o_ref, h_scr, *, shift_row):
    i = pl.program_id(1)
    tm = x_ref.shape[1]
    dff = o_ref.shape[2]
    rows = tm + 2 * HALO_ROWS
    m = mod_ref[0]
    shift, scale = m[shift_row:shift_row + 1], m[shift_row + 1:shift_row + 2]
    g = g_ref[...]
    h_scr[0:HALO_ROWS] = _norm_mod(xp_ref[0], g, shift, scale).astype(BF16)
    h_scr[HALO_ROWS:HALO_ROWS + tm] = _norm_mod(x_ref[0], g, shift, scale).astype(BF16)
    h_scr[HALO_ROWS + tm:] = _norm_mod(xn_ref[0], g, shift, scale).astype(BF16)
    top = jnp.where(i > 0, 1.0, 0.0)
    bottom = jnp.where(i < pl.num_programs(1) - 1, 1.0, 0.0)
    h = h_scr[...]
    for c0 in range(0, dff, FFN_COLS):
        gate = jnp.dot(h, w_ref[:, c0:c0 + FFN_COLS], preferred_element_type=F32)
        gate = jnp.concatenate([gate[:HALO_ROWS] * top, gate[HALO_ROWS:HALO_ROWS + tm],
                                gate[HALO_ROWS + tm:] * bottom], axis=0)
        val = jnp.dot(h, w_ref[:, dff + c0:dff + c0 + FFN_COLS], preferred_element_type=F32)
        cw = 0.5 * cw_ref[:, c0:c0 + FFN_COLS]
        half = 0.5 * cb_ref[:, c0:c0 + FFN_COLS] + gate * cw[1:2]
        half = half + pltpu.roll(gate, 1, 0) * cw[0:1]
        half = half + pltpu.roll(gate, rows - 1, 0) * cw[2:3]
        act = half * (1.0 + jnp.tanh(half)) * val
        o_ref[0, :, c0:c0 + FFN_COLS] = act[HALO_ROWS:HALO_ROWS + tm].astype(o_ref.dtype)


def _ffn_up(x, g_all, mod_all, w_all, conv_w_all, conv_b_all, layer, *, shift_row, tm=1024):
    bsz, t, d = x.shape
    dff = w_all.shape[2] // 2
    hb = tm // HALO_ROWS
    last = t // HALO_ROWS - 1
    return pl.pallas_call(
        functools.partial(_ffn_up_kernel, shift_row=shift_row),
        grid=(bsz, t // tm),
        in_specs=[
            pl.BlockSpec((1, HALO_ROWS, d), lambda b, i: (b, jnp.maximum(i * hb - 1, 0), 0)),
            pl.BlockSpec((1, tm, d), lambda b, i: (b, i, 0)),
            pl.BlockSpec((1, HALO_ROWS, d), lambda b, i: (b, jnp.minimum((i + 1) * hb, last), 0)),
            pl.BlockSpec((None, 1, d), lambda b, i: (layer, 0, 0)),
            pl.BlockSpec((None, 1, 6, d), lambda b, i: (layer, b, 0, 0)),
            pl.BlockSpec((None, d, 2 * dff), lambda b, i: (layer, 0, 0), pipeline_mode=pl.Buffered(1)),
            pl.BlockSpec((None, FFN_CONV, dff), lambda b, i: (layer, 0, 0)),
            pl.BlockSpec((None, 1, dff), lambda b, i: (layer, 0, 0)),
        ],
        out_specs=pl.BlockSpec((1, tm, dff), lambda b, i: (b, i, 0)),
        out_shape=jax.ShapeDtypeStruct((bsz, t, dff), BF16),
        scratch_shapes=[pltpu.VMEM((tm + 2 * HALO_ROWS, d), BF16)],
        compiler_params=_params(("arbitrary", "arbitrary")),
        name="ffn_up_conv",
    )(x, x, x, g_all, mod_all, w_all, conv_w_all, conv_b_all)


def _ffn_down_kernel(a_ref, w_ref, x_ref, mod_ref, *rest, gate_row):
    y = jnp.dot(a_ref[0], w_ref[...], preferred_element_type=F32)
    x = x_ref[0] + mod_ref[0][gate_row:gate_row + 1] * y
    if len(rest) == 2:
        g_ref, o_ref = rest
        x = x * lax.rsqrt(jnp.mean(x * x, axis=-1, keepdims=True) + RMS_EPS) * g_ref[...]
    else:
        o_ref, = rest
    o_ref[0] = x


def _ffn_down(a, w_all, x, mod_all, layer, final_g, *, gate_row, tm=512):
    bsz, t, d = x.shape
    dff = a.shape[-1]
    extra = [] if final_g is None else [final_g]
    return pl.pallas_call(
        functools.partial(_ffn_down_kernel, gate_row=gate_row),
        grid=(bsz, t // tm),
        in_specs=[
            pl.BlockSpec((1, tm, dff), lambda b, i: (b, i, 0)),
            pl.BlockSpec((None, dff, d), lambda b, i: (layer, 0, 0)),
            pl.BlockSpec((1, tm, d), lambda b, i: (b, i, 0)),
            pl.BlockSpec((None, 1, 6, d), lambda b, i: (layer, b, 0, 0)),
        ] + [pl.BlockSpec((1, d), lambda b, i: (0, 0)) for _ in extra],
        out_specs=pl.BlockSpec((1, tm, d), lambda b, i: (b, i, 0)),
        out_shape=jax.ShapeDtypeStruct((bsz, t, d), F32),
        compiler_params=_params(("arbitrary", "arbitrary")),
        name="ffn_down",
    )(a, w_all, x, mod_all, *extra)


def kernel(x, c, ada_w, ada_b, norm_mix_g, norm_ffn_g, even_w_in, even_w_out, hgrn_lb_logits, hgrn_norm_g, na_rpb, odd_w_in, odd_w_out, pool_w, pool_scale, ffn_w_up, ffn_conv_w, ffn_conv_b, ffn_w_down, final_norm_g):
    bsz, t, d = x.shape
    depth = ada_w.shape[0]
    a_width = hgrn_lb_logits.shape[1]
    n_groups = len(C_CONFIGS)
    c_width = C_HPG * ATT_DH
    n_c = 3 * n_groups * c_width

    lb_soft = jax.nn.softmax(hgrn_lb_logits.astype(F32), axis=0)
    lower_bounds = jnp.cumsum(lb_soft, axis=0) - lb_soft[0]
    slopes = jnp.exp2(-8.0 * jnp.arange(1, n_groups * C_HPG + 1, dtype=F32) / (n_groups * C_HPG))
    slopes = slopes.reshape(n_groups, C_HPG)

    mod_all = _ada_mod(c, ada_w, ada_b).reshape(depth, bsz, 6, d)

    g_mix, g_ffn = norm_mix_g.reshape(depth, 1, d), norm_ffn_g.reshape(depth, 1, d)
    even_in, odd_in = even_w_in.astype(BF16), odd_w_in.astype(BF16)
    even_out, odd_out = even_w_out.astype(BF16), odd_w_out.astype(BF16)
    w_up, w_down = ffn_w_up.astype(BF16), ffn_w_down.astype(BF16)
    conv_b = ffn_conv_b.reshape(depth, 1, -1)
    qkv = lambda g: tuple(((k * n_groups + g) * c_width, c_width) for k in range(3))

    for l in range(depth):
        if l % 2 == 0:
            e = l // 2
            b_width = (even_in.shape[2] - 5 * a_width) // 3
            b_cols = tuple((5 * a_width + k * b_width, b_width) for k in range(3))
            z_a, z_b = _in_proj(x, g_mix, mod_all, even_in, l, e, (((0, 5 * a_width),), b_cols),
                                (F32, BF16), tm=512)
            o_a = _hgrn2(z_a, lower_bounds[e], hgrn_norm_g[e])
            o_b = _neighbourhood_attention(z_b, na_rpb[e])
            x = _out_proj([o_a, o_b], even_out, e, x, mod_all, l, gate_row=2, odd=False)
        else:
            o_i = l // 2
            outs, lses = [], []
            for g, (_, dil) in enumerate(C_CONFIGS):
                if dil == 1:
                    z_g, z_d = _in_proj(x, g_mix, mod_all, odd_in, l, o_i, (qkv(g), ((n_c, odd_in.shape[2] - n_c),)),
                                        (BF16, F32), tm=DIL_TILE[dil])
                else:
                    z_g, = _in_proj(x, g_mix, mod_all, odd_in, l, o_i, (qkv(g),), (BF16,),
                                    tm=DIL_TILE[dil], perm=dil)
                o_g, lse_g = _dilated_group(z_g, dil, slopes[g], DIL_TILE[dil])
                outs.append(o_g)
                lses.append(lse_g)
            o_d = _multiscale_pool(z_d, pool_w[o_i], pool_scale[o_i])
            x = _out_proj(outs + lses + [o_d], odd_out, o_i, x, mod_all, l, gate_row=2, odd=True)
        a = _ffn_up(x, g_ffn, mod_all, w_up, ffn_conv_w, conv_b, l, shift_row=3)
        final_g = final_norm_g.reshape(1, d) if l == depth - 1 else None
        x = _ffn_down(a, w_down, x, mod_all, l, final_g, gate_row=5)
    return x
```

```python
import functools

import jax
import jax.numpy as jnp
import numpy as np
from jax import lax
from jax.experimental import pallas as pl
from jax.experimental.pallas import tpu as pltpu

F32 = jnp.float32
BF16 = jnp.bfloat16

GRID_W = 64
RMS_EPS = 1e-6
LB_FLOOR = 1e-12
MASK_VALUE = -1e30
A_DK = 128
HGRN_TILE = 128
HGRN_GROUP = 4
HGRN_LEVELS = (4, 8, 16, 32, 64)
ATT_DH = 64
ATT_SLAB = 256
ATT_HPS = ATT_SLAB // ATT_DH
NA_KR = 8
NA_KC = 16
NA_ROWS_PER_STEP = 8
C_CONFIGS = ((128, 1), (512, 4), (2048, 16))
C_HPG = 8
DIL_TILE = {1: 1024, 4: 1024, 16: 2048}
ATT_SUB = 128
ATT_HALO = 64
POOL_WINDOWS = (2, 4, 8, 16)
POOL_HALO = 8
FFN_CONV = 3
FFN_COLS = 256
HALO_ROWS = 8
LANES = 128
VMEM_LIMIT = 56 * 1024 * 1024


def _params(sem):
    return pltpu.CompilerParams(dimension_semantics=sem, vmem_limit_bytes=VMEM_LIMIT)


def _silu(v):
    return v * (1.0 / (1.0 + jnp.exp(-v)))


def _norm_mod(x, g, shift, scale):
    ms = jnp.mean(x * x, axis=-1, keepdims=True)
    return x * lax.rsqrt(ms + RMS_EPS) * (g * (1.0 + scale)) + shift


def _ada_kernel(c_ref, w_ref, b_ref, o_ref):
    c = c_ref[...]
    ca = _silu(c).astype(BF16)
    o_ref[0] = jnp.dot(ca, w_ref[0].astype(BF16), preferred_element_type=F32) + b_ref[0]


def _ada_mod(c, ada_w, ada_b):
    depth, d, n = ada_w.shape
    bsz = c.shape[0]
    tn = 1024
    return pl.pallas_call(
        _ada_kernel,
        grid=(depth, n // tn),
        in_specs=[
            pl.BlockSpec((bsz, d), lambda l, j: (0, 0)),
            pl.BlockSpec((1, d, tn), lambda l, j: (l, 0, j)),
            pl.BlockSpec((1, 1, tn), lambda l, j: (l, 0, j)),
        ],
        out_specs=pl.BlockSpec((1, bsz, tn), lambda l, j: (l, 0, j)),
        out_shape=jax.ShapeDtypeStruct((depth, bsz, n), F32),
        compiler_params=_params(("arbitrary", "arbitrary")),
        name="ada_mod",
    )(c, ada_w, ada_b.reshape(depth, 1, n))


def _in_proj_kernel(x_ref, g_ref, mod_ref, *rest, shift_row, perm, col_groups):
    n_out = len(col_groups)
    n_w = sum(len(ranges) for ranges in col_groups)
    w_refs, rest = rest[:n_w], rest[n_w:]
    o_refs, h_ref = rest[:n_out], rest[n_out]
    m = mod_ref[0]
    h = _norm_mod(x_ref[0], g_ref[...], m[shift_row:shift_row + 1], m[shift_row + 1:shift_row + 2])
    if perm == 1:
        h_ref[...] = h.astype(BF16)
    else:
        h32 = rest[n_out + 1]
        cs = h.shape[0] // perm
        for j in range(h32.shape[0]):
            h32[j] = h[:, j * LANES:(j + 1) * LANES]
        for r in range(perm):
            for j in range(h32.shape[0]):
                h_ref[r * cs:(r + 1) * cs, j * LANES:(j + 1) * LANES] = (
                    h32[j, pl.ds(r, cs, stride=perm), :].astype(BF16))
    hb = h_ref[...]
    w_iter = iter(w_refs)
    for o_ref, ranges in zip(o_refs, col_groups):
        off = 0
        for _, width in ranges:
            o_ref[0, :, off:off + width] = jnp.dot(hb, next(w_iter)[...],
                                                   preferred_element_type=F32).astype(o_ref.dtype)
            off += width


def _in_proj(x, g_all, mod_all, w_all, layer, w_layer, col_groups, out_dtypes, *, tm=1024, perm=1):
    bsz, t, d = x.shape
    widths = [sum(w for _, w in ranges) for ranges in col_groups]
    w_specs = []
    for ranges in col_groups:
        for start, width in ranges:
            assert start % width == 0
            w_specs.append(pl.BlockSpec((None, d, width), lambda b, i, blk=start // width: (w_layer, 0, blk),
                                        pipeline_mode=pl.Buffered(1)))
    return pl.pallas_call(
        functools.partial(_in_proj_kernel, shift_row=0, perm=perm, col_groups=col_groups),
        grid=(bsz, t // tm),
        in_specs=[
            pl.BlockSpec((1, tm, d), lambda b, i: (b, i, 0)),
            pl.BlockSpec((None, 1, d), lambda b, i: (layer, 0, 0)),
            pl.BlockSpec((None, 1, 6, d), lambda b, i: (layer, b, 0, 0)),
        ] + w_specs,
        out_specs=[pl.BlockSpec((1, tm, w), lambda b, i: (b, i, 0)) for w in widths],
        out_shape=[jax.ShapeDtypeStruct((bsz, t, w), dt) for w, dt in zip(widths, out_dtypes)],
        scratch_shapes=[pltpu.VMEM((tm, d), BF16)] + ([pltpu.VMEM((d // LANES, tm, LANES), F32)] if perm > 1 else []),
        compiler_params=_params(("arbitrary", "arbitrary")),
        name="in_proj",
    )(x, g_all, mod_all, *([w_all] * len(w_specs)))


def _split3(x):
    hi = x.astype(BF16)
    r = x - hi.astype(F32)
    mid = r.astype(BF16)
    lo = (r - mid.astype(F32)).astype(BF16)
    return hi, mid, lo


def _span_code(later, earlier):
    g = HGRN_GROUP
    code = jnp.where(((later // g) == (earlier // g)) & (later >= earlier), later - earlier, -1)
    for li, m in enumerate(HGRN_LEVELS):
        straddles = ((later // (2 * m)) == (earlier // (2 * m))) & ((later % (2 * m)) >= m) & ((earlier % (2 * m)) < m)
        code = jnp.where(straddles, g + li, code)
    return code


def _hgrn_kernel(q_ref, ff_ref, fb_ref, v_ref, g_ref, lb_ref, ng_ref, o_ref, acc_scr, qs_scr):
    t = q_ref.shape[1]
    tl = HGRN_TILE
    g8 = HGRN_GROUP
    n_tiles = t // tl
    log2e = 1.4426950408889634

    lb = lb_ref[...]
    lb_floor = jnp.maximum(lb, LB_FLOOR)
    one_m_lb = 1.0 - lb
    log2_one_m_lb = jnp.log2(one_m_lb)

    row = lax.broadcasted_iota(jnp.int32, (tl, tl), 0)
    col = lax.broadcasted_iota(jnp.int32, (tl, tl), 1)
    tri = jnp.where(col <= row, 1.0, 0.0).astype(BF16)
    code_f = _span_code(row, col)
    code_b = _span_code(col, row)
    r2 = lax.broadcasted_iota(jnp.int32, (2 * tl, 2 * tl), 0) < tl
    c2 = lax.broadcasted_iota(jnp.int32, (2 * tl, 2 * tl), 1) < tl
    ones_bd = jnp.where(r2 == c2, 1.0, 0.0).astype(BF16)

    def gates(fz):
        e = jnp.exp(-jnp.abs(fz))
        r = 1.0 / (1.0 + e)
        pos = fz >= 0
        sig_pos = jnp.where(pos, r, e * r)
        sig_neg = jnp.where(pos, e * r, r)
        log2_sig_neg = -(jnp.maximum(fz, 0.0) * log2e + jnp.log2(1.0 + e))
        return jnp.log2(lb_floor + one_m_lb * sig_pos), one_m_lb * sig_neg, log2_one_m_lb + log2_sig_neg

    def rot_group(x, shift):
        return pltpu.roll(x.reshape(tl // HALO_ROWS, HALO_ROWS, x.shape[1]), shift % HALO_ROWS, 1).reshape(x.shape)

    def span_ref(b, fwd, m):
        blocks = []
        for a in range(0, tl, 2 * m):
            edge = a + m - 1 if fwd else a + m
            blocks.append(jnp.broadcast_to(b[edge:edge + 1], (2 * m, b.shape[1])))
        return jnp.concatenate(blocks, axis=0)

    acc_scr[...] = jnp.zeros_like(acc_scr)

    def silu_body(i, carry):
        t0 = pl.multiple_of(i * tl, tl)
        qs_scr[pl.ds(t0, tl), :] = _silu(q_ref[0, pl.ds(t0, tl), :])
        return carry

    lax.fori_loop(0, n_tiles, silu_body, 0)

    def body(n, carry):
        st_f, st_b = carry
        tf = pl.multiple_of(n * tl, tl)
        tb = pl.multiple_of((n_tiles - 1 - n) * tl, tl)
        q_f = qs_scr[pl.ds(tf, tl), :]
        q_b = qs_scr[pl.ds(tb, tl), :]
        v_f = v_ref[0, pl.ds(tf, tl), :].astype(BF16)
        v_b = v_ref[0, pl.ds(tb, tl), :].astype(BF16)
        lf_f, kk_f, lk_f = gates(ff_ref[0, pl.ds(tf, tl), :])
        lf_b, kk_b, lk_b = gates(fb_ref[0, pl.ds(tb, tl), :])

        pieces = jnp.concatenate(_split3(lf_f) + _split3(lf_b), axis=1)
        ps = jnp.dot(tri, pieces, preferred_element_type=F32)
        b_f = ps[:, 0:A_DK] + ps[:, A_DK:2 * A_DK] + ps[:, 2 * A_DK:3 * A_DK]
        p_b = ps[:, 3 * A_DK:4 * A_DK] + ps[:, 4 * A_DK:5 * A_DK] + ps[:, 5 * A_DK:6 * A_DK]
        b_b = p_b[tl - 1:tl] - p_b + lf_b

        w_f, w_b = lk_f - b_f, lk_b - b_b
        lhs = []
        for e in range(g8):
            wf_e = w_f if e == 0 else rot_group(w_f, e)
            wb_e = w_b if e == 0 else rot_group(w_b, HALO_ROWS - e)
            pf = q_f * jnp.exp2(jnp.minimum(b_f + wf_e, 0.0))
            pb = q_b * jnp.exp2(jnp.minimum(b_b + wb_e, 0.0))
            lhs.append(jnp.concatenate([pf, pb], axis=1).astype(BF16))
        half = g8 // 2
        att = [jnp.dot(jnp.concatenate(part, axis=0), ones_bd, preferred_element_type=F32)
               for part in (lhs[:half], lhs[half:])]
        a_f = jnp.zeros((tl, tl), F32)
        a_b = jnp.zeros((tl, tl), F32)
        for e in range(g8):
            blk = att[e // half][(e % half) * tl:(e % half + 1) * tl]
            a_f = jnp.where(code_f == e, blk[:, :A_DK], a_f)
            a_b = jnp.where(code_b == e, blk[:, A_DK:], a_b)

        for li, m in enumerate(HGRN_LEVELS):
            e_f = jnp.exp2(-jnp.abs(b_f - span_ref(b_f, True, m)))
            e_b = jnp.exp2(-jnp.abs(b_b - span_ref(b_b, False, m)))
            s_f = lax.dot_general((q_f * e_f).astype(BF16), (kk_f * e_f).astype(BF16),
                                  (((1,), (1,)), ((), ())), preferred_element_type=F32)
            s_b = lax.dot_general((q_b * e_b).astype(BF16), (kk_b * e_b).astype(BF16),
                                  (((1,), (1,)), ((), ())), preferred_element_type=F32)
            a_f = jnp.where(code_f == g8 + li, s_f, a_f)
            a_b = jnp.where(code_b == g8 + li, s_b, a_b)

        def inter(q, v_bf, kk, b, edge, st):
            b_edge = b[edge:edge + 1]
            qb = (q * jnp.exp2(b)).astype(BF16)
            o = lax.dot_general(qb, st.astype(BF16), (((1,), (1,)), ((), ())), preferred_element_type=F32)
            kd = (kk * jnp.exp2(b_edge - b)).astype(BF16)
            kv = lax.dot_general(v_bf, kd, (((0,), (0,)), ((), ())), preferred_element_type=F32)
            return o, st * jnp.exp2(b_edge) + kv

        oi_f, st_f = inter(q_f, v_f, kk_f, b_f, tl - 1, st_f)
        oi_b, st_b = inter(q_b, v_b, kk_b, b_b, 0, st_b)
        acc_scr[pl.ds(tf, tl), :] += jnp.dot(a_f.astype(BF16), v_f, preferred_element_type=F32) + oi_f
        acc_scr[pl.ds(tb, tl), :] += jnp.dot(a_b.astype(BF16), v_b, preferred_element_type=F32) + oi_b
        return st_f, st_b

    zero = jnp.zeros((A_DK, A_DK), F32)
    lax.fori_loop(0, n_tiles, body, (zero, zero), unroll=4)

    ng = ng_ref[...]

    def out_body(i, carry):
        t0 = pl.multiple_of(i * tl, tl)
        o = acc_scr[pl.ds(t0, tl), :]
        o = o * lax.rsqrt(jnp.mean(o * o, axis=-1, keepdims=True) + RMS_EPS) * ng
        o_ref[0, pl.ds(t0, tl), :] = (o * _silu(g_ref[0, pl.ds(t0, tl), :])).astype(o_ref.dtype)
        return carry

    lax.fori_loop(0, n_tiles, out_body, 0)


def _hgrn2(z, lb, norm_g):
    bsz, t, w5 = z.shape
    heads = w5 // 5 // A_DK

    def col(k):
        return pl.BlockSpec((1, t, A_DK), lambda b, h, k=k: (b, 0, h + heads * k))

    return pl.pallas_call(
        _hgrn_kernel,
        grid=(bsz, heads),
        in_specs=[col(0), col(1), col(2), col(3), col(4),
                  pl.BlockSpec((1, A_DK), lambda b, h: (0, h)),
                  pl.BlockSpec((1, A_DK), lambda b, h: (0, 0))],
        out_specs=pl.BlockSpec((1, t, A_DK), lambda b, h: (b, 0, h)),
        out_shape=jax.ShapeDtypeStruct((bsz, t, heads * A_DK), BF16),
        scratch_shapes=[pltpu.VMEM((t, A_DK), F32)] * 2,
        compiler_params=_params(("arbitrary", "arbitrary")),
        name="hgrn2",
    )(z, z, z, z, z, lb.reshape(1, heads * A_DK), norm_g.reshape(1, A_DK))


def _stacked_attention(q, k, v, bias, extra_ok):
    m_rows = q.shape[0]
    heads = q.shape[1] // ATT_DH
    head_of_lane = lax.broadcasted_iota(jnp.int32, (1, q.shape[1]), 1) // ATT_DH
    zero = jnp.zeros((), q.dtype)
    q_bd = jnp.concatenate([jnp.where(head_of_lane == h, q, zero) for h in range(heads)], axis=0)
    s = lax.dot_general(q_bd, k, (((1,), (1,)), ((), ())), preferred_element_type=F32) + bias
    if extra_ok is not None:
        s = jnp.where(extra_ok, s, MASK_VALUE)
    m = jnp.max(s, axis=-1, keepdims=True)
    p = jnp.exp(s - m)
    den = jnp.sum(p, axis=-1, keepdims=True)
    o = jnp.dot(p.astype(BF16), v, preferred_element_type=F32) / den
    lse = m + jnp.log(den)
    out = o[0:m_rows]
    lse_b = jnp.broadcast_to(lse[0:m_rows], out.shape)
    for h in range(1, heads):
        sel = head_of_lane == h
        out = jnp.where(sel, o[h * m_rows:(h + 1) * m_rows], out)
        lse_b = jnp.where(sel, lse[h * m_rows:(h + 1) * m_rows], lse_b)
    return out, lse_b


def _na_kernel(q_ref, k_ref, v_ref, *rest, rows, kr):
    bias_refs, o_ref = rest[:-1], rest[-1]
    nk = kr * GRID_W
    nb = ATT_HPS * GRID_W
    scale = ATT_DH ** -0.5
    for j, bias_ref in enumerate(bias_refs):
        r = pl.program_id(1) * len(bias_refs) + j
        rs = jnp.clip(r - kr // 2, 0, rows - kr)
        k0 = pl.multiple_of(rs * GRID_W, GRID_W)
        q = q_ref[0, j * GRID_W:(j + 1) * GRID_W, :] * scale
        for s in range(q.shape[-1] // ATT_SLAB):
            sl = slice(s * ATT_SLAB, (s + 1) * ATT_SLAB)
            out, _ = _stacked_attention(q[:, sl], k_ref[0, pl.ds(k0, nk), sl], v_ref[0, pl.ds(k0, nk), sl],
                                        bias_ref[0, s * nb:(s + 1) * nb, :], None)
            o_ref[0, j * GRID_W:(j + 1) * GRID_W, sl] = out.astype(o_ref.dtype)


def _na_bias_table(rpb_all, rows):
    layers, heads = rpb_all.shape[:2]
    kr = min(NA_KR, rows)
    col = np.arange(GRID_W)
    cs = np.clip(col - NA_KC // 2, 0, GRID_W - NA_KC)
    kc = np.arange(GRID_W)
    rel = kc[None, :] - col[:, None] + NA_KC - 1
    ok = (kc[None, :] >= cs[:, None]) & (kc[None, :] < cs[:, None] + NA_KC)
    pick_col = ((rel[None] == np.arange(2 * NA_KC - 1)[:, None, None]) & ok[None]).astype(np.float32)
    di = np.arange(kr)[None, :] - np.arange(kr)[:, None] + NA_KR - 1
    pick_row = (di[:, :, None] == np.arange(2 * NA_KR - 1)[None, None, :]).astype(np.float32)
    table = jnp.einsum('lhdr,vid,rck->lvhcik', rpb_all.astype(F32), pick_row, pick_col,
                       precision=lax.Precision.HIGHEST)
    table = jnp.where(ok[None, None, None, :, None, :], table, MASK_VALUE)
    return table.reshape(layers, kr, heads * GRID_W, kr * GRID_W)


def _neighbourhood_attention(zb, table, layer):
    bsz, t, w3 = zb.shape
    width = w3 // 3
    heads = width // ATT_DH
    rows = t // GRID_W
    kr = min(NA_KR, rows)
    rps = NA_ROWS_PER_STEP

    def bias_spec(j):
        def index(b, i):
            r = i * rps + j
            return (layer, r - jnp.clip(r - kr // 2, 0, rows - kr), 0, 0)
        return pl.BlockSpec((None, 1, heads * GRID_W, kr * GRID_W), index)

    return pl.pallas_call(
        functools.partial(_na_kernel, rows=rows, kr=kr),
        grid=(bsz, rows // rps),
        in_specs=[
            pl.BlockSpec((1, rps * GRID_W, width), lambda b, i: (b, i, 0)),
            pl.BlockSpec((1, t, width), lambda b, i: (b, 0, 1)),
            pl.BlockSpec((1, t, width), lambda b, i: (b, 0, 2)),
        ] + [bias_spec(j) for j in range(rps)],
        out_specs=pl.BlockSpec((1, rps * GRID_W, width), lambda b, i: (b, i, 0)),
        out_shape=jax.ShapeDtypeStruct((bsz, t, width), BF16),
        compiler_params=_params(("arbitrary", "arbitrary")),
        name="neighbourhood_attention",
    )(zb, zb, zb, *([table] * rps))


def _dil_kernel(q_ref, kp_ref, k_ref, kn_ref, vp_ref, v_ref, vn_ref, bias_ref, o_ref, lse_ref,
                k_scr, v_scr, *, seq_len):
    i = pl.program_id(1)
    dil, cs = q_ref.shape[2], q_ref.shape[3]
    width = q_ref.shape[4]
    scale = ATT_DH ** -0.5
    nk = ATT_SUB + 2 * ATT_HALO
    nb = ATT_HPS * ATT_SUB
    kcol = lax.broadcasted_iota(jnp.int32, (1, nk), 1)
    for r in range(dil):
        k_scr[0:ATT_HALO] = kp_ref[0, 0, r]
        k_scr[ATT_HALO:ATT_HALO + cs] = k_ref[0, 0, r]
        k_scr[ATT_HALO + cs:] = kn_ref[0, 0, r]
        v_scr[0:ATT_HALO] = vp_ref[0, 0, r]
        v_scr[ATT_HALO:ATT_HALO + cs] = v_ref[0, 0, r]
        v_scr[ATT_HALO + cs:] = vn_ref[0, 0, r]
        for sb in range(cs // ATT_SUB):
            r0 = sb * ATT_SUB
            kpos = i * cs + r0 - ATT_HALO + kcol
            ok = (kpos >= 0) & (kpos < seq_len)
            q = q_ref[0, 0, r, r0:r0 + ATT_SUB, :] * scale
            if dil == 1:
                rows_out = pl.ds(r0, ATT_SUB)
            else:
                rows_out = pl.ds(r0 * dil + r, ATT_SUB, stride=dil)
            for s in range(width // ATT_SLAB):
                sl = slice(s * ATT_SLAB, (s + 1) * ATT_SLAB)
                out, lse = _stacked_attention(q[:, sl], k_scr[r0:r0 + nk, sl], v_scr[r0:r0 + nk, sl],
                                              bias_ref[s * nb:(s + 1) * nb, :], ok)
                for j in range(ATT_SLAB // LANES):
                    ch = s * (ATT_SLAB // LANES) + j
                    o_ref[0, ch, rows_out, :] = out[:, j * LANES:(j + 1) * LANES]
                    lse_ref[0, ch, rows_out, :] = lse[:, j * LANES:(j + 1) * LANES]


def _alibi_band(slopes_g, dil):
    rel = np.arange(ATT_SUB + 2 * ATT_HALO)[None, :] - ATT_HALO - np.arange(ATT_SUB)[:, None]
    band = np.abs(rel) <= ATT_HALO
    dist = jnp.asarray(np.abs(rel) * dil, F32)
    bias = jnp.where(band[None], -slopes_g[:, None, None] * dist[None], MASK_VALUE)
    return bias.reshape(slopes_g.shape[0] * ATT_SUB, ATT_SUB + 2 * ATT_HALO)


def _dilated_group(zp, dil, slopes_g, tm):
    bsz, t, f = zp.shape
    width = f // 3
    cs = tm // dil
    nt = t // tm
    zv = zp.reshape(bsz, nt, dil, cs, f)
    bias = _alibi_band(slopes_g, dil)
    tail = cs // ATT_HALO - 1

    def main(off):
        return pl.BlockSpec((1, 1, dil, cs, width), lambda b, i: (b, i, 0, 0, off))

    def prev(off):
        return pl.BlockSpec((1, 1, dil, ATT_HALO, width), lambda b, i: (b, jnp.maximum(i - 1, 0), 0, tail, off))

    def nxt(off):
        return pl.BlockSpec((1, 1, dil, ATT_HALO, width), lambda b, i: (b, jnp.minimum(i + 1, nt - 1), 0, 0, off))

    out_spec = pl.BlockSpec((1, width // LANES, tm, LANES), lambda b, i: (b, 0, i, 0))
    return pl.pallas_call(
        functools.partial(_dil_kernel, seq_len=t // dil),
        grid=(bsz, nt),
        in_specs=[main(0), prev(1), main(1), nxt(1), prev(2), main(2), nxt(2),
                  pl.BlockSpec(bias.shape, lambda b, i: (0, 0))],
        out_specs=[out_spec, out_spec],
        out_shape=[jax.ShapeDtypeStruct((bsz, width // LANES, t, LANES), F32)] * 2,
        scratch_shapes=[pltpu.VMEM((cs + 2 * ATT_HALO, width), BF16)] * 2,
        compiler_params=_params(("arbitrary", "arbitrary")),
        name=f"dilated_attention_d{dil}",
    )(zv, zv, zv, zv, zv, zv, zv, bias)


def _pool_kernel(up_ref, u_ref, un_ref, w_ref, sc_ref, o_ref, scr, *, seq_len):
    i = pl.program_id(1)
    tm = u_ref.shape[1]
    gc = w_ref.shape[1]
    scr[0:HALO_ROWS] = jnp.where(i > 0, up_ref[0], 0.0)
    scr[HALO_ROWS:HALO_ROWS + tm] = u_ref[0]
    scr[HALO_ROWS + tm:] = jnp.where(i < pl.num_programs(1) - 1, un_ref[0], 0.0)
    tpos = i * tm + lax.broadcasted_iota(jnp.int32, (tm, 1), 0)
    for g, w in enumerate(POOL_WINDOWS):
        sl = slice(g * gc, (g + 1) * gc)
        tot = jnp.zeros((tm, gc), F32)
        for off in range(-(w // 2), w - w // 2):
            tot = tot + scr[HALO_ROWS + off:HALO_ROWS + off + tm, sl]
        lo = jnp.clip(tpos - w // 2, 0, seq_len)
        hi = jnp.clip(tpos + w - w // 2, 0, seq_len)
        mean = tot / (hi - lo).astype(F32)
        diff = mean - scr[HALO_ROWS:HALO_ROWS + tm, sl]
        y = jnp.dot(diff.astype(BF16), w_ref[g], preferred_element_type=F32)
        o_ref[0, :, sl] = (y * sc_ref[:, sl]).astype(o_ref.dtype)


def _multiscale_pool(u, w_groups, scale, tm=512):
    bsz, t, ch = u.shape
    hb = tm // HALO_ROWS
    last = t // HALO_ROWS - 1
    return pl.pallas_call(
        functools.partial(_pool_kernel, seq_len=t),
        grid=(bsz, t // tm),
        in_specs=[
            pl.BlockSpec((1, HALO_ROWS, ch), lambda b, i: (b, jnp.maximum(i * hb - 1, 0), 0)),
            pl.BlockSpec((1, tm, ch), lambda b, i: (b, i, 0)),
            pl.BlockSpec((1, HALO_ROWS, ch), lambda b, i: (b, jnp.minimum((i + 1) * hb, last), 0)),
            pl.BlockSpec(w_groups.shape, lambda b, i: (0, 0, 0)),
            pl.BlockSpec((1, ch), lambda b, i: (0, 0)),
        ],
        out_specs=pl.BlockSpec((1, tm, ch), lambda b, i: (b, i, 0)),
        out_shape=jax.ShapeDtypeStruct((bsz, t, ch), BF16),
        scratch_shapes=[pltpu.VMEM((tm + 2 * HALO_ROWS, ch), F32)],
        compiler_params=_params(("arbitrary", "arbitrary")),
        name="multiscale_pool",
    )(u, u, u, w_groups.astype(BF16), scale.reshape(1, ch))


def _out_even_kernel(oa_ref, ob_ref, wa_ref, wb_ref, x_ref, mod_ref, o_ref, *, gate_row):
    mixed = jnp.dot(oa_ref[0], wa_ref[...], preferred_element_type=F32)
    mixed = mixed + jnp.dot(ob_ref[0], wb_ref[...], preferred_element_type=F32)
    gate = mod_ref[0][gate_row:gate_row + 1]
    o_ref[0] = x_ref[0] + gate * mixed


def _out_odd_kernel(o0_ref, o1_ref, o2_ref, l0_ref, l1_ref, l2_ref, od_ref, wa_ref, wb_ref, x_ref, mod_ref,
                    o_ref, *, gate_row):
    chunks = []
    for j in range(o0_ref.shape[1]):
        l0, l1, l2 = l0_ref[0, j], l1_ref[0, j], l2_ref[0, j]
        m = jnp.maximum(jnp.maximum(l0, l1), l2)
        e0, e1, e2 = jnp.exp(l0 - m), jnp.exp(l1 - m), jnp.exp(l2 - m)
        chunks.append((e0 * o0_ref[0, j] + e1 * o1_ref[0, j] + e2 * o2_ref[0, j]) / (e0 + e1 + e2))
    oc = jnp.concatenate(chunks, axis=1)
    mixed = jnp.dot(oc.astype(BF16), wa_ref[...], preferred_element_type=F32)
    mixed = mixed + jnp.dot(od_ref[0], wb_ref[...], preferred_element_type=F32)
    gate = mod_ref[0][gate_row:gate_row + 1]
    o_ref[0] = x_ref[0] + gate * mixed


def _out_proj(parts, w_all, w_layer, x, mod_all, layer, *, gate_row, odd, tm=512):
    bsz, t, d = x.shape
    half = w_all.shape[1] // 2
    tok = lambda width: pl.BlockSpec((1, tm, width), lambda b, i: (b, i, 0))
    chunked = pl.BlockSpec((1, half // LANES, tm, LANES), lambda b, i: (b, 0, i, 0))
    kern = _out_odd_kernel if odd else _out_even_kernel
    part_specs = [chunked if p.ndim == 4 else tok(half) for p in parts]
    return pl.pallas_call(
        functools.partial(kern, gate_row=gate_row),
        grid=(bsz, t // tm),
        in_specs=part_specs + [
            pl.BlockSpec((None, half, d), lambda b, i: (w_layer, 0, 0)),
            pl.BlockSpec((None, half, d), lambda b, i: (w_layer, 1, 0)),
            tok(d),
            pl.BlockSpec((None, 1, 6, d), lambda b, i: (layer, b, 0, 0)),
        ],
        out_specs=tok(d),
        out_shape=jax.ShapeDtypeStruct((bsz, t, d), F32),
        compiler_params=_params(("arbitrary", "arbitrary")),
        name="out_proj_odd" if odd else "out_proj_even",
    )(*parts, w_all, w_all, x, mod_all)


def _ffn_up_kernel(xp_ref, x_ref, xn_ref, g_ref, mod_ref, w_ref, cw_ref, cb_ref, o_ref, h_scr, *, shift_row):
    i = pl.program_id(1)
    tm = x_ref.shape[1]
    dff = o_ref.shape[2]
    rows = tm + 2 * HALO_ROWS
    m = mod_ref[0]
    shift, scale = m[shift_row:shift_row + 1], m[shift_row + 1:shift_row + 2]
    g = g_ref[...]
    h_scr[0:HALO_ROWS] = _norm_mod(xp_ref[0], g, shift, scale).astype(BF16)
    h_scr[HALO_ROWS:HALO_ROWS + tm] = _norm_mod(x_ref[0], g, shift, scale).astype(BF16)
    h_scr[HALO_ROWS + tm:] = _norm_mod(xn_ref[0], g, shift, scale).astype(BF16)
    top = jnp.where(i > 0, 1.0, 0.0)
    bottom = jnp.where(i < pl.num_programs(1) - 1, 1.0, 0.0)
    h = h_scr[...]
    for c0 in range(0, dff, FFN_COLS):
        gate = jnp.dot(h, w_ref[:, c0:c0 + FFN_COLS], preferred_element_type=F32)
        gate = jnp.concatenate([gate[:HALO_ROWS] * top, gate[HALO_ROWS:HALO_ROWS + tm],
                                gate[HALO_ROWS + tm:] * bottom], axis=0)
        val = jnp.dot(h, w_ref[:, dff + c0:dff + c0 + FFN_COLS], preferred_element_type=F32)
        cw = 0.5 * cw_ref[:, c0:c0 + FFN_COLS]
        half = 0.5 * cb_ref[:, c0:c0 + FFN_COLS] + gate * cw[1:2]
        half = half + pltpu.roll(gate, 1, 0) * cw[0:1]
        half = half + pltpu.roll(gate, rows - 1, 0) * cw[2:3]
        act = half * (1.0 + jnp.tanh(half)) * val
        o_ref[0, :, c0:c0 + FFN_COLS] = act[HALO_ROWS:HALO_ROWS + tm].astype(o_ref.dtype)


def _ffn_up(x, g_all, mod_all, w_all, conv_w_all, conv_b_all, layer, *, shift_row, tm=1024):
    bsz, t, d = x.shape
    dff = w_all.shape[2] // 2
    hb = tm // HALO_ROWS
    last = t // HALO_ROWS - 1
    return pl.pallas_call(
        functools.partial(_ffn_up_kernel, shift_row=shift_row),
        grid=(bsz, t // tm),
        in_specs=[
            pl.BlockSpec((1, HALO_ROWS, d), lambda b, i: (b, jnp.maximum(i * hb - 1, 0), 0)),
            pl.BlockSpec((1, tm, d), lambda b, i: (b, i, 0)),
            pl.BlockSpec((1, HALO_ROWS, d), lambda b, i: (b, jnp.minimum((i + 1) * hb, last), 0)),
            pl.BlockSpec((None, 1, d), lambda b, i: (layer, 0, 0)),
            pl.BlockSpec((None, 1, 6, d), lambda b, i: (layer, b, 0, 0)),
            pl.BlockSpec((None, d, 2 * dff), lambda b, i: (layer, 0, 0), pipeline_mode=pl.Buffered(1)),
            pl.BlockSpec((None, FFN_CONV, dff), lambda b, i: (layer, 0, 0)),
            pl.BlockSpec((None, 1, dff), lambda b, i: (layer, 0, 0)),
        ],
        out_specs=pl.BlockSpec((1, tm, dff), lambda b, i: (b, i, 0)),
        out_shape=jax.ShapeDtypeStruct((bsz, t, dff), BF16),
        scratch_shapes=[pltpu.VMEM((tm + 2 * HALO_ROWS, d), BF16)],
        compiler_params=_params(("arbitrary", "arbitrary")),
        name="ffn_up_conv",
    )(x, x, x, g_all, mod_all, w_all, conv_w_all, conv_b_all)


def _ffn_down_kernel(a_ref, w_ref, x_ref, mod_ref, *rest, gate_row):
    y = jnp.dot(a_ref[0], w_ref[...], preferred_element_type=F32)
    x = x_ref[0] + mod_ref[0][gate_row:gate_row + 1] * y
    if len(rest) == 2:
        g_ref, o_ref = rest
        x = x * lax.rsqrt(jnp.mean(x * x, axis=-1, keepdims=True) + RMS_EPS) * g_ref[...]
    else:
        o_ref, = rest
    o_ref[0] = x


def _ffn_down(a, w_all, x, mod_all, layer, final_g, *, gate_row, tm=512):
    bsz, t, d = x.shape
    dff = a.shape[-1]
    extra = [] if final_g is None else [final_g]
    return pl.pallas_call(
        functools.partial(_ffn_down_kernel, gate_row=gate_row),
        grid=(bsz, t // tm),
        in_specs=[
            pl.BlockSpec((1, tm, dff), lambda b, i: (b, i, 0)),
            pl.BlockSpec((None, dff, d), lambda b, i: (layer, 0, 0)),
            pl.BlockSpec((1, tm, d), lambda b, i: (b, i, 0)),
            pl.BlockSpec((None, 1, 6, d), lambda b, i: (layer, b, 0, 0)),
        ] + [pl.BlockSpec((1, d), lambda b, i: (0, 0)) for _ in extra],
        out_specs=pl.BlockSpec((1, tm, d), lambda b, i: (b, i, 0)),
        out_shape=jax.ShapeDtypeStruct((bsz, t, d), F32),
        compiler_params=_params(("arbitrary", "arbitrary")),
        name="ffn_down",
    )(a, w_all, x, mod_all, *extra)


def kernel(x, c, ada_w, ada_b, norm_mix_g, norm_ffn_g, even_w_in, even_w_out, hgrn_lb_logits, hgrn_norm_g, na_rpb, odd_w_in, odd_w_out, pool_w, pool_scale, ffn_w_up, ffn_conv_w, ffn_conv_b, ffn_w_down, final_norm_g):
    bsz, t, d = x.shape
    depth = ada_w.shape[0]
    a_width = hgrn_lb_logits.shape[1]
    n_groups = len(C_CONFIGS)
    c_width = C_HPG * ATT_DH
    n_c = 3 * n_groups * c_width

    lb_soft = jax.nn.softmax(hgrn_lb_logits.astype(F32), axis=0)
    lower_bounds = jnp.cumsum(lb_soft, axis=0) - lb_soft[0]
    slopes = jnp.exp2(-8.0 * jnp.arange(1, n_groups * C_HPG + 1, dtype=F32) / (n_groups * C_HPG))
    slopes = slopes.reshape(n_groups, C_HPG)

    mod_all = _ada_mod(c, ada_w, ada_b).reshape(depth, bsz, 6, d)

    g_mix, g_ffn = norm_mix_g.reshape(depth, 1, d), norm_ffn_g.reshape(depth, 1, d)
    even_in, odd_in = even_w_in.astype(BF16), odd_w_in.astype(BF16)
    even_out, odd_out = even_w_out.astype(BF16), odd_w_out.astype(BF16)
    w_up, w_down = ffn_w_up.astype(BF16), ffn_w_down.astype(BF16)
    conv_b = ffn_conv_b.reshape(depth, 1, -1)
    na_tables = _na_bias_table(na_rpb, t // GRID_W)
    qkv = lambda g: tuple(((k * n_groups + g) * c_width, c_width) for k in range(3))

    for l in range(depth):
        if l % 2 == 0:
            e = l // 2
            b_width = (even_in.shape[2] - 5 * a_width) // 3
            b_cols = tuple((5 * a_width + k * b_width, b_width) for k in range(3))
            z_a, z_b = _in_proj(x, g_mix, mod_all, even_in, l, e, (((0, 5 * a_width),), b_cols),
                                (F32, BF16), tm=512)
            o_a = _hgrn2(z_a, lower_bounds[e], hgrn_norm_g[e])
            o_b = _neighbourhood_attention(z_b, na_tables, e)
            x = _out_proj([o_a, o_b], even_out, e, x, mod_all, l, gate_row=2, odd=False)
        else:
            o_i = l // 2
            outs, lses = [], []
            for g, (_, dil) in enumerate(C_CONFIGS):
                if dil == 1:
                    z_g, z_d = _in_proj(x, g_mix, mod_all, odd_in, l, o_i, (qkv(g), ((n_c, odd_in.shape[2] - n_c),)),
                                        (BF16, F32), tm=DIL_TILE[dil])
                else:
                    z_g, = _in_proj(x, g_mix, mod_all, odd_in, l, o_i, (qkv(g),), (BF16,),
                                    tm=DIL_TILE[dil], perm=dil)
                o_g, lse_g = _dilated_group(z_g, dil, slopes[g], DIL_TILE[dil])
                outs.append(o_g)
                lses.append(lse_g)
            o_d = _multiscale_pool(z_d, pool_w[o_i], pool_scale[o_i])
            x = _out_proj(outs + lses + [o_d], odd_out, o_i, x, mod_all, l, gate_row=2, odd=True)
        a = _ffn_up(x, g_ffn, mod_all, w_up, ffn_conv_w, conv_b, l, shift_row=3)
        final_g = final_norm_g.reshape(1, d) if l == depth - 1 else None
        x = _ffn_down(a, w_down, x, mod_all, l, final_g, gate_row=5)
    return x
```

```python
import functools

import jax
import jax.numpy as jnp
import numpy as np
from jax import lax
from jax.experimental import pallas as pl
from jax.experimental.pallas import tpu as pltpu

F32 = jnp.float32
BF16 = jnp.bfloat16

GRID_W = 64
RMS_EPS = 1e-6
LB_FLOOR = 1e-12
MASK_VALUE = -1e30
A_DK = 128
HGRN_TILE = 128
HGRN_GROUP = 4
HGRN_LEVELS = (4, 8, 16, 32, 64)
ATT_DH = 64
ATT_SLAB = 256
ATT_HPS = ATT_SLAB // ATT_DH
NA_KR = 8
NA_KC = 16
NA_ROWS_PER_STEP = 8
C_CONFIGS = ((128, 1), (512, 4), (2048, 16))
C_HPG = 8
DIL_TILE = {1: 1024, 4: 1024, 16: 2048}
ATT_SUB = 128
ATT_HALO = 64
POOL_WINDOWS = (2, 4, 8, 16)
POOL_HALO = 8
FFN_CONV = 3
FFN_COLS = 256
HALO_ROWS = 8
LANES = 128
VMEM_LIMIT = 56 * 1024 * 1024


def _params(sem):
    return pltpu.CompilerParams(dimension_semantics=sem, vmem_limit_bytes=VMEM_LIMIT)


def _silu(v):
    return v * (1.0 / (1.0 + jnp.exp(-v)))


def _norm_mod(x, g, shift, scale):
    ms = jnp.mean(x * x, axis=-1, keepdims=True)
    return x * lax.rsqrt(ms + RMS_EPS) * (g * (1.0 + scale)) + shift


def _ada_kernel(c_ref, w_ref, b_ref, o_ref):
    c = c_ref[...]
    ca = _silu(c).astype(BF16)
    o_ref[0] = jnp.dot(ca, w_ref[0].astype(BF16), preferred_element_type=F32) + b_ref[0]


def _ada_mod(c, ada_w, ada_b):
    depth, d, n = ada_w.shape
    bsz = c.shape[0]
    tn = 1024
    return pl.pallas_call(
        _ada_kernel,
        grid=(depth, n // tn),
        in_specs=[
            pl.BlockSpec((bsz, d), lambda l, j: (0, 0)),
            pl.BlockSpec((1, d, tn), lambda l, j: (l, 0, j)),
            pl.BlockSpec((1, 1, tn), lambda l, j: (l, 0, j)),
        ],
        out_specs=pl.BlockSpec((1, bsz, tn), lambda l, j: (l, 0, j)),
        out_shape=jax.ShapeDtypeStruct((depth, bsz, n), F32),
        compiler_params=_params(("arbitrary", "arbitrary")),
        name="ada_mod",
    )(c, ada_w, ada_b.reshape(depth, 1, n))


def _in_proj_kernel(x_ref, g_ref, mod_ref, *rest, shift_row, perm, col_groups):
    n_out = len(col_groups)
    n_w = sum(len(ranges) for ranges in col_groups)
    w_refs, rest = rest[:n_w], rest[n_w:]
    o_refs, h_ref = rest[:n_out], rest[n_out]
    m = mod_ref[0]
    h = _norm_mod(x_ref[0], g_ref[...], m[shift_row:shift_row + 1], m[shift_row + 1:shift_row + 2])
    tm = h.shape[0]
    if perm == 1:
        h_ref[...] = h.astype(BF16)
        parts = 1
    else:
        h32 = rest[n_out + 1]
        for j in range(h32.shape[0]):
            h32[j] = h[:, j * LANES:(j + 1) * LANES]
        parts = 2
    cs = tm // perm
    rows = tm // parts
    for p in range(parts):
        if perm > 1:
            for r in range(p * perm // parts, (p + 1) * perm // parts):
                for j in range(h32.shape[0]):
                    h_ref[r * cs:(r + 1) * cs, j * LANES:(j + 1) * LANES] = (
                        h32[j, pl.ds(r, cs, stride=perm), :].astype(BF16))
        hb = h_ref[p * rows:(p + 1) * rows]
        w_iter = iter(w_refs)
        for o_ref, ranges in zip(o_refs, col_groups):
            off = 0
            for _, width in ranges:
                o_ref[0, p * rows:(p + 1) * rows, off:off + width] = jnp.dot(
                    hb, next(w_iter)[...], preferred_element_type=F32).astype(o_ref.dtype)
                off += width


def _in_proj(x, g_all, mod_all, w_all, layer, w_layer, col_groups, out_dtypes, *, tm=1024, perm=1):
    bsz, t, d = x.shape
    widths = [sum(w for _, w in ranges) for ranges in col_groups]
    w_specs = []
    for ranges in col_groups:
        for start, width in ranges:
            assert start % width == 0
            w_specs.append(pl.BlockSpec((None, d, width), lambda b, i, blk=start // width: (w_layer, 0, blk),
                                        pipeline_mode=pl.Buffered(1)))
    return pl.pallas_call(
        functools.partial(_in_proj_kernel, shift_row=0, perm=perm, col_groups=col_groups),
        grid=(bsz, t // tm),
        in_specs=[
            pl.BlockSpec((1, tm, d), lambda b, i: (b, i, 0)),
            pl.BlockSpec((None, 1, d), lambda b, i: (layer, 0, 0)),
            pl.BlockSpec((None, 1, 6, d), lambda b, i: (layer, b, 0, 0)),
        ] + w_specs,
        out_specs=[pl.BlockSpec((1, tm, w), lambda b, i: (b, i, 0)) for w in widths],
        out_shape=[jax.ShapeDtypeStruct((bsz, t, w), dt) for w, dt in zip(widths, out_dtypes)],
        scratch_shapes=[pltpu.VMEM((tm, d), BF16)] + ([pltpu.VMEM((d // LANES, tm, LANES), F32)] if perm > 1 else []),
        compiler_params=_params(("arbitrary", "arbitrary")),
        name="in_proj",
    )(x, g_all, mod_all, *([w_all] * len(w_specs)))


def _split3(x):
    hi = x.astype(BF16)
    r = x - hi.astype(F32)
    mid = r.astype(BF16)
    lo = (r - mid.astype(F32)).astype(BF16)
    return hi, mid, lo


def _span_code(later, earlier):
    g = HGRN_GROUP
    code = jnp.where(((later // g) == (earlier // g)) & (later >= earlier), later - earlier, -1)
    for li, m in enumerate(HGRN_LEVELS):
        straddles = ((later // (2 * m)) == (earlier // (2 * m))) & ((later % (2 * m)) >= m) & ((earlier % (2 * m)) < m)
        code = jnp.where(straddles, g + li, code)
    return code


def _hgrn_kernel(q_ref, ff_ref, fb_ref, v_ref, g_ref, lb_ref, ng_ref, o_ref, acc_scr, qs_scr):
    t = q_ref.shape[1]
    tl = HGRN_TILE
    g8 = HGRN_GROUP
    n_tiles = t // tl
    log2e = 1.4426950408889634

    lb = lb_ref[...]
    lb_floor = jnp.maximum(lb, LB_FLOOR)
    one_m_lb = 1.0 - lb
    log2_one_m_lb = jnp.log2(one_m_lb)

    row = lax.broadcasted_iota(jnp.int32, (tl, tl), 0)
    col = lax.broadcasted_iota(jnp.int32, (tl, tl), 1)
    tri = jnp.where(col <= row, 1.0, 0.0).astype(BF16)
    code_f = _span_code(row, col)
    code_b = _span_code(col, row)
    r2 = lax.broadcasted_iota(jnp.int32, (2 * tl, 2 * tl), 0) < tl
    c2 = lax.broadcasted_iota(jnp.int32, (2 * tl, 2 * tl), 1) < tl
    ones_bd = jnp.where(r2 == c2, 1.0, 0.0).astype(BF16)

    def gates(fz):
        e = jnp.exp(-jnp.abs(fz))
        r = 1.0 / (1.0 + e)
        pos = fz >= 0
        sig_pos = jnp.where(pos, r, e * r)
        sig_neg = jnp.where(pos, e * r, r)
        log2_sig_neg = -(jnp.maximum(fz, 0.0) * log2e + jnp.log2(1.0 + e))
        return jnp.log2(lb_floor + one_m_lb * sig_pos), one_m_lb * sig_neg, log2_one_m_lb + log2_sig_neg

    def rot_group(x, shift):
        return pltpu.roll(x.reshape(tl // HALO_ROWS, HALO_ROWS, x.shape[1]), shift % HALO_ROWS, 1).reshape(x.shape)

    def span_ref(b, fwd, m):
        blocks = []
        for a in range(0, tl, 2 * m):
            edge = a + m - 1 if fwd else a + m
            blocks.append(jnp.broadcast_to(b[edge:edge + 1], (2 * m, b.shape[1])))
        return jnp.concatenate(blocks, axis=0)

    acc_scr[...] = jnp.zeros_like(acc_scr)

    def silu_body(i, carry):
        t0 = pl.multiple_of(i * tl, tl)
        qs_scr[pl.ds(t0, tl), :] = _silu(q_ref[0, pl.ds(t0, tl), :])
        return carry

    lax.fori_loop(0, n_tiles, silu_body, 0)

    def body(n, carry):
        st_f, st_b = carry
        tf = pl.multiple_of(n * tl, tl)
        tb = pl.multiple_of((n_tiles - 1 - n) * tl, tl)
        q_f = qs_scr[pl.ds(tf, tl), :]
        q_b = qs_scr[pl.ds(tb, tl), :]
        v_f = v_ref[0, pl.ds(tf, tl), :].astype(BF16)
        v_b = v_ref[0, pl.ds(tb, tl), :].astype(BF16)
        lf_f, kk_f, lk_f = gates(ff_ref[0, pl.ds(tf, tl), :])
        lf_b, kk_b, lk_b = gates(fb_ref[0, pl.ds(tb, tl), :])

        pieces = jnp.concatenate(_split3(lf_f) + _split3(lf_b), axis=1)
        ps = jnp.dot(tri, pieces, preferred_element_type=F32)
        b_f = ps[:, 0:A_DK] + ps[:, A_DK:2 * A_DK] + ps[:, 2 * A_DK:3 * A_DK]
        p_b = ps[:, 3 * A_DK:4 * A_DK] + ps[:, 4 * A_DK:5 * A_DK] + ps[:, 5 * A_DK:6 * A_DK]
        b_b = p_b[tl - 1:tl] - p_b + lf_b

        w_f, w_b = lk_f - b_f, lk_b - b_b
        lhs = []
        for e in range(g8):
            wf_e = w_f if e == 0 else rot_group(w_f, e)
            wb_e = w_b if e == 0 else rot_group(w_b, HALO_ROWS - e)
            pf = q_f * jnp.exp2(jnp.minimum(b_f + wf_e, 0.0))
            pb = q_b * jnp.exp2(jnp.minimum(b_b + wb_e, 0.0))
            lhs.append(jnp.concatenate([pf, pb], axis=1).astype(BF16))
        half = g8 // 2
        att = [jnp.dot(jnp.concatenate(part, axis=0), ones_bd, preferred_element_type=F32)
               for part in (lhs[:half], lhs[half:])]
        a_f = jnp.zeros((tl, tl), F32)
        a_b = jnp.zeros((tl, tl), F32)
        for e in range(g8):
            blk = att[e // half][(e % half) * tl:(e % half + 1) * tl]
            a_f = jnp.where(code_f == e, blk[:, :A_DK], a_f)
            a_b = jnp.where(code_b == e, blk[:, A_DK:], a_b)

        for li, m in enumerate(HGRN_LEVELS):
            e_f = jnp.exp2(-jnp.abs(b_f - span_ref(b_f, True, m)))
            e_b = jnp.exp2(-jnp.abs(b_b - span_ref(b_b, False, m)))
            s_f = lax.dot_general((q_f * e_f).astype(BF16), (kk_f * e_f).astype(BF16),
                                  (((1,), (1,)), ((), ())), preferred_element_type=F32)
            s_b = lax.dot_general((q_b * e_b).astype(BF16), (kk_b * e_b).astype(BF16),
                                  (((1,), (1,)), ((), ())), preferred_element_type=F32)
            a_f = jnp.where(code_f == g8 + li, s_f, a_f)
            a_b = jnp.where(code_b == g8 + li, s_b, a_b)

        def inter(q, v_bf, kk, b, edge, st):
            b_edge = b[edge:edge + 1]
            qb = (q * jnp.exp2(b)).astype(BF16)
            o = lax.dot_general(qb, st.astype(BF16), (((1,), (1,)), ((), ())), preferred_element_type=F32)
            kd = (kk * jnp.exp2(b_edge - b)).astype(BF16)
            kv = lax.dot_general(v_bf, kd, (((0,), (0,)), ((), ())), preferred_element_type=F32)
            return o, st * jnp.exp2(b_edge) + kv

        oi_f, st_f = inter(q_f, v_f, kk_f, b_f, tl - 1, st_f)
        oi_b, st_b = inter(q_b, v_b, kk_b, b_b, 0, st_b)
        acc_scr[pl.ds(tf, tl), :] += jnp.dot(a_f.astype(BF16), v_f, preferred_element_type=F32) + oi_f
        acc_scr[pl.ds(tb, tl), :] += jnp.dot(a_b.astype(BF16), v_b, preferred_element_type=F32) + oi_b
        return st_f, st_b

    zero = jnp.zeros((A_DK, A_DK), F32)
    lax.fori_loop(0, n_tiles, body, (zero, zero), unroll=4)

    ng = ng_ref[...]

    def out_body(i, carry):
        t0 = pl.multiple_of(i * tl, tl)
        o = acc_scr[pl.ds(t0, tl), :]
        o = o * lax.rsqrt(jnp.mean(o * o, axis=-1, keepdims=True) + RMS_EPS) * ng
        o_ref[0, pl.ds(t0, tl), :] = (o * _silu(g_ref[0, pl.ds(t0, tl), :])).astype(o_ref.dtype)
        return carry

    lax.fori_loop(0, n_tiles, out_body, 0)


def _hgrn2(z, lb, norm_g):
    bsz, t, w5 = z.shape
    heads = w5 // 5 // A_DK

    def col(k):
        return pl.BlockSpec((1, t, A_DK), lambda b, h, k=k: (b, 0, h + heads * k))

    return pl.pallas_call(
        _hgrn_kernel,
        grid=(bsz, heads),
        in_specs=[col(0), col(1), col(2), col(3), col(4),
                  pl.BlockSpec((1, A_DK), lambda b, h: (0, h)),
                  pl.BlockSpec((1, A_DK), lambda b, h: (0, 0))],
        out_specs=pl.BlockSpec((1, t, A_DK), lambda b, h: (b, 0, h)),
        out_shape=jax.ShapeDtypeStruct((bsz, t, heads * A_DK), BF16),
        scratch_shapes=[pltpu.VMEM((t, A_DK), F32)] * 2,
        compiler_params=_params(("arbitrary", "arbitrary")),
        name="hgrn2",
    )(z, z, z, z, z, lb.reshape(1, heads * A_DK), norm_g.reshape(1, A_DK))


def _stacked_attention(q, k, v, bias, extra_ok):
    m_rows = q.shape[0]
    heads = q.shape[1] // ATT_DH
    head_of_lane = lax.broadcasted_iota(jnp.int32, (1, q.shape[1]), 1) // ATT_DH
    zero = jnp.zeros((), q.dtype)
    q_bd = jnp.concatenate([jnp.where(head_of_lane == h, q, zero) for h in range(heads)], axis=0)
    s = lax.dot_general(q_bd, k, (((1,), (1,)), ((), ())), preferred_element_type=F32) + bias
    if extra_ok is not None:
        s = jnp.where(extra_ok, s, MASK_VALUE)
    m = jnp.max(s, axis=-1, keepdims=True)
    p = jnp.exp(s - m)
    den = jnp.sum(p, axis=-1, keepdims=True)
    o = jnp.dot(p.astype(BF16), v, preferred_element_type=F32) / den
    lse = m + jnp.log(den)
    out = o[0:m_rows]
    lse_b = jnp.broadcast_to(lse[0:m_rows], out.shape)
    for h in range(1, heads):
        sel = head_of_lane == h
        out = jnp.where(sel, o[h * m_rows:(h + 1) * m_rows], out)
        lse_b = jnp.where(sel, lse[h * m_rows:(h + 1) * m_rows], lse_b)
    return out, lse_b


def _na_kernel(q_ref, k_ref, v_ref, *rest, rows, kr):
    bias_refs, o_ref = rest[:-1], rest[-1]
    nk = kr * GRID_W
    nb = ATT_HPS * GRID_W
    scale = ATT_DH ** -0.5
    for j, bias_ref in enumerate(bias_refs):
        r = pl.program_id(1) * len(bias_refs) + j
        rs = jnp.clip(r - kr // 2, 0, rows - kr)
        k0 = pl.multiple_of(rs * GRID_W, GRID_W)
        q = q_ref[0, j * GRID_W:(j + 1) * GRID_W, :] * scale
        for s in range(q.shape[-1] // ATT_SLAB):
            sl = slice(s * ATT_SLAB, (s + 1) * ATT_SLAB)
            out, _ = _stacked_attention(q[:, sl], k_ref[0, pl.ds(k0, nk), sl], v_ref[0, pl.ds(k0, nk), sl],
                                        bias_ref[0, s * nb:(s + 1) * nb, :], None)
            o_ref[0, j * GRID_W:(j + 1) * GRID_W, sl] = out.astype(o_ref.dtype)


def _na_bias_table(rpb_all, rows):
    layers, heads = rpb_all.shape[:2]
    kr = min(NA_KR, rows)
    col = np.arange(GRID_W)
    cs = np.clip(col - NA_KC // 2, 0, GRID_W - NA_KC)
    kc = np.arange(GRID_W)
    rel = kc[None, :] - col[:, None] + NA_KC - 1
    ok = (kc[None, :] >= cs[:, None]) & (kc[None, :] < cs[:, None] + NA_KC)
    pick_col = ((rel[None] == np.arange(2 * NA_KC - 1)[:, None, None]) & ok[None]).astype(np.float32)
    di = np.arange(kr)[None, :] - np.arange(kr)[:, None] + NA_KR - 1
    pick_row = (di[:, :, None] == np.arange(2 * NA_KR - 1)[None, None, :]).astype(np.float32)
    table = jnp.einsum('lhdr,vid,rck->lvhcik', rpb_all.astype(F32), pick_row, pick_col,
                       precision=lax.Precision.HIGHEST)
    table = jnp.where(ok[None, None, None, :, None, :], table, MASK_VALUE)
    return table.reshape(layers, kr, heads * GRID_W, kr * GRID_W)


def _neighbourhood_attention(zb, table, layer):
    bsz, t, w3 = zb.shape
    width = w3 // 3
    heads = width // ATT_DH
    rows = t // GRID_W
    kr = min(NA_KR, rows)
    rps = NA_ROWS_PER_STEP

    def bias_spec(j):
        def index(b, i):
            r = i * rps + j
            return (layer, r - jnp.clip(r - kr // 2, 0, rows - kr), 0, 0)
        return pl.BlockSpec((None, 1, heads * GRID_W, kr * GRID_W), index)

    return pl.pallas_call(
        functools.partial(_na_kernel, rows=rows, kr=kr),
        grid=(bsz, rows // rps),
        in_specs=[
            pl.BlockSpec((1, rps * GRID_W, width), lambda b, i: (b, i, 0)),
            pl.BlockSpec((1, t, width), lambda b, i: (b, 0, 1)),
            pl.BlockSpec((1, t, width), lambda b, i: (b, 0, 2)),
        ] + [bias_spec(j) for j in range(rps)],
        out_specs=pl.BlockSpec((1, rps * GRID_W, width), lambda b, i: (b, i, 0)),
        out_shape=jax.ShapeDtypeStruct((bsz, t, width), BF16),
        compiler_params=_params(("arbitrary", "arbitrary")),
        name="neighbourhood_attention",
    )(zb, zb, zb, *([table] * rps))


def _dil_kernel(q_ref, kp_ref, k_ref, kn_ref, vp_ref, v_ref, vn_ref, bias_ref, *rest, has_prev, emit_lse):
    n_in = 2 if has_prev else 0
    n_out = 2 if emit_lse else 1
    prev_refs, out_refs = rest[:n_in], rest[n_in:n_in + n_out]
    k_scr, v_scr = rest[n_in + n_out:]
    i = pl.program_id(1)
    last_tile = pl.num_programs(1) - 1
    dil, cs = q_ref.shape[2], q_ref.shape[3]
    width = q_ref.shape[4]
    scale = ATT_DH ** -0.5
    nk = ATT_SUB + 2 * ATT_HALO
    nb = ATT_HPS * ATT_SUB
    n_sub = cs // ATT_SUB
    for r in range(dil):
        k_scr[0:ATT_HALO] = kp_ref[0, 0, r]
        k_scr[ATT_HALO:ATT_HALO + cs] = k_ref[0, 0, r]
        k_scr[ATT_HALO + cs:] = kn_ref[0, 0, r]
        v_scr[0:ATT_HALO] = vp_ref[0, 0, r]
        v_scr[ATT_HALO:ATT_HALO + cs] = v_ref[0, 0, r]
        v_scr[ATT_HALO + cs:] = vn_ref[0, 0, r]
        for sb in range(n_sub):
            r0 = sb * ATT_SUB
            edge = jnp.int32(0)
            if sb == 0:
                edge = edge + jnp.where(i == 0, 1, 0)
            if sb == n_sub - 1:
                edge = edge + jnp.where(i == last_tile, 2, 0)
            q = q_ref[0, 0, r, r0:r0 + ATT_SUB, :] * scale
            if dil == 1:
                rows_out = pl.ds(r0, ATT_SUB)
            else:
                rows_out = pl.ds(r0 * dil + r, ATT_SUB, stride=dil)
            for s in range(width // ATT_SLAB):
                sl = slice(s * ATT_SLAB, (s + 1) * ATT_SLAB)
                out, lse = _stacked_attention(q[:, sl], k_scr[r0:r0 + nk, sl], v_scr[r0:r0 + nk, sl],
                                              bias_ref[edge, s * nb:(s + 1) * nb, :], None)
                for j in range(ATT_SLAB // LANES):
                    ch = s * (ATT_SLAB // LANES) + j
                    o_j, l_j = out[:, j * LANES:(j + 1) * LANES], lse[:, j * LANES:(j + 1) * LANES]
                    if has_prev:
                        o_p, l_p = prev_refs[0][0, ch, rows_out, :], prev_refs[1][0, ch, rows_out, :]
                        m = jnp.maximum(l_p, l_j)
                        e_p, e_j = jnp.exp(l_p - m), jnp.exp(l_j - m)
                        den = e_p + e_j
                        o_j = (e_p * o_p + e_j * o_j) / den
                        l_j = m + jnp.log(den)
                    out_refs[0][0, ch, rows_out, :] = o_j
                    if emit_lse:
                        out_refs[1][0, ch, rows_out, :] = l_j


def _alibi_band(slopes_g, dil):
    nk = ATT_SUB + 2 * ATT_HALO
    rel = np.arange(nk)[None, :] - ATT_HALO - np.arange(ATT_SUB)[:, None]
    band = np.abs(rel) <= ATT_HALO
    dist = jnp.asarray(np.abs(rel) * dil, F32)
    bias = jnp.where(band[None], -slopes_g[:, None, None] * dist[None], MASK_VALUE)
    bias = bias.reshape(slopes_g.shape[0] * ATT_SUB, nk)
    kcol = np.arange(nk)
    before, after = kcol < ATT_HALO, kcol >= ATT_HALO + ATT_SUB
    cut = lambda cols: jnp.where(cols[None, :], MASK_VALUE, bias)
    return jnp.stack([bias, cut(before), cut(after), cut(before | after)])


def _dilated_group(zp, dil, slopes_g, tm, prev, emit_lse):
    bsz, t, f = zp.shape
    width = f // 3
    cs = tm // dil
    nt = t // tm
    zv = zp.reshape(bsz, nt, dil, cs, f)
    bias = _alibi_band(slopes_g, dil)
    tail = cs // ATT_HALO - 1
    prev = () if prev is None else tuple(prev)

    def main(off):
        return pl.BlockSpec((1, 1, dil, cs, width), lambda b, i: (b, i, 0, 0, off))

    def before(off):
        return pl.BlockSpec((1, 1, dil, ATT_HALO, width), lambda b, i: (b, jnp.maximum(i - 1, 0), 0, tail, off))

    def after(off):
        return pl.BlockSpec((1, 1, dil, ATT_HALO, width), lambda b, i: (b, jnp.minimum(i + 1, nt - 1), 0, 0, off))

    tok_spec = pl.BlockSpec((1, width // LANES, tm, LANES), lambda b, i: (b, 0, i, 0))
    n_out = 2 if emit_lse else 1
    return pl.pallas_call(
        functools.partial(_dil_kernel, has_prev=bool(prev), emit_lse=emit_lse),
        grid=(bsz, nt),
        in_specs=[main(0), before(1), main(1), after(1), before(2), main(2), after(2),
                  pl.BlockSpec(bias.shape, lambda b, i: (0, 0, 0))] + [tok_spec] * len(prev),
        out_specs=[tok_spec] * n_out,
        out_shape=[jax.ShapeDtypeStruct((bsz, width // LANES, t, LANES), F32)] * n_out,
        scratch_shapes=[pltpu.VMEM((cs + 2 * ATT_HALO, width), BF16)] * 2,
        compiler_params=_params(("arbitrary", "arbitrary")),
        name=f"dilated_attention_d{dil}",
    )(zv, zv, zv, zv, zv, zv, zv, bias, *prev)


def _pool_kernel(up_ref, u_ref, un_ref, w_ref, sc_ref, o_ref, scr, *, seq_len):
    i = pl.program_id(1)
    tm = u_ref.shape[1]
    gc = w_ref.shape[1]
    scr[0:HALO_ROWS] = jnp.where(i > 0, up_ref[0], 0.0)
    scr[HALO_ROWS:HALO_ROWS + tm] = u_ref[0]
    scr[HALO_ROWS + tm:] = jnp.where(i < pl.num_programs(1) - 1, un_ref[0], 0.0)
    tpos = i * tm + lax.broadcasted_iota(jnp.int32, (tm, 1), 0)
    for g, w in enumerate(POOL_WINDOWS):
        sl = slice(g * gc, (g + 1) * gc)
        tot = jnp.zeros((tm, gc), F32)
        for off in range(-(w // 2), w - w // 2):
            tot = tot + scr[HALO_ROWS + off:HALO_ROWS + off + tm, sl]
        lo = jnp.clip(tpos - w // 2, 0, seq_len)
        hi = jnp.clip(tpos + w - w // 2, 0, seq_len)
        mean = tot / (hi - lo).astype(F32)
        diff = mean - scr[HALO_ROWS:HALO_ROWS + tm, sl]
        y = jnp.dot(diff.astype(BF16), w_ref[g], preferred_element_type=F32)
        o_ref[0, :, sl] = (y * sc_ref[:, sl]).astype(o_ref.dtype)


def _multiscale_pool(u, w_groups, scale, tm=512):
    bsz, t, ch = u.shape
    hb = tm // HALO_ROWS
    last = t // HALO_ROWS - 1
    return pl.pallas_call(
        functools.partial(_pool_kernel, seq_len=t),
        grid=(bsz, t // tm),
        in_specs=[
            pl.BlockSpec((1, HALO_ROWS, ch), lambda b, i: (b, jnp.maximum(i * hb - 1, 0), 0)),
            pl.BlockSpec((1, tm, ch), lambda b, i: (b, i, 0)),
            pl.BlockSpec((1, HALO_ROWS, ch), lambda b, i: (b, jnp.minimum((i + 1) * hb, last), 0)),
            pl.BlockSpec(w_groups.shape, lambda b, i: (0, 0, 0)),
            pl.BlockSpec((1, ch), lambda b, i: (0, 0)),
        ],
        out_specs=pl.BlockSpec((1, tm, ch), lambda b, i: (b, i, 0)),
        out_shape=jax.ShapeDtypeStruct((bsz, t, ch), BF16),
        scratch_shapes=[pltpu.VMEM((tm + 2 * HALO_ROWS, ch), F32)],
        compiler_params=_params(("arbitrary", "arbitrary")),
        name="multiscale_pool",
    )(u, u, u, w_groups.astype(BF16), scale.reshape(1, ch))


def _out_even_kernel(oa_ref, ob_ref, wa_ref, wb_ref, x_ref, mod_ref, o_ref, *, gate_row):
    mixed = jnp.dot(oa_ref[0], wa_ref[...], preferred_element_type=F32)
    mixed = mixed + jnp.dot(ob_ref[0], wb_ref[...], preferred_element_type=F32)
    gate = mod_ref[0][gate_row:gate_row + 1]
    o_ref[0] = x_ref[0] + gate * mixed


def _out_odd_kernel(oc_ref, od_ref, wa_ref, wb_ref, x_ref, mod_ref, o_ref, *, gate_row):
    oc = jnp.concatenate([oc_ref[0, j] for j in range(oc_ref.shape[1])], axis=1)
    mixed = jnp.dot(oc.astype(BF16), wa_ref[...], preferred_element_type=F32)
    mixed = mixed + jnp.dot(od_ref[0], wb_ref[...], preferred_element_type=F32)
    gate = mod_ref[0][gate_row:gate_row + 1]
    o_ref[0] = x_ref[0] + gate * mixed


def _out_proj(parts, w_all, w_layer, x, mod_all, layer, *, gate_row, odd, tm=1024):
    bsz, t, d = x.shape
    half = w_all.shape[1] // 2
    tok = lambda width: pl.BlockSpec((1, tm, width), lambda b, i: (b, i, 0))
    chunked = pl.BlockSpec((1, half // LANES, tm, LANES), lambda b, i: (b, 0, i, 0))
    kern = _out_odd_kernel if odd else _out_even_kernel
    part_specs = [chunked if p.ndim == 4 else tok(half) for p in parts]
    return pl.pallas_call(
        functools.partial(kern, gate_row=gate_row),
        grid=(bsz, t // tm),
        in_specs=part_specs + [
            pl.BlockSpec((None, half, d), lambda b, i: (w_layer, 0, 0)),
            pl.BlockSpec((None, half, d), lambda b, i: (w_layer, 1, 0)),
            tok(d),
            pl.BlockSpec((None, 1, 6, d), lambda b, i: (layer, b, 0, 0)),
        ],
        out_specs=tok(d),
        out_shape=jax.ShapeDtypeStruct((bsz, t, d), F32),
        compiler_params=_params(("arbitrary", "arbitrary")),
        name="out_proj_odd" if odd else "out_proj_even",
    )(*parts, w_all, w_all, x, mod_all)


def _ffn_up_kernel(xp_ref, x_ref, xn_ref, g_ref, mod_ref, w_ref, cw_ref, cb_ref, o_ref, h_scr, *, shift_row):
    i = pl.program_id(1)
    tm = x_ref.shape[1]
    dff = o_ref.shape[2]
    rows = tm + 2 * HALO_ROWS
    m = mod_ref[0]
    shift, scale = m[shift_row:shift_row + 1], m[shift_row + 1:shift_row + 2]
    g = g_ref[...]
    h_scr[0:HALO_ROWS] = _norm_mod(xp_ref[0], g, shift, scale).astype(BF16)
    h_scr[HALO_ROWS:HALO_ROWS + tm] = _norm_mod(x_ref[0], g, shift, scale).astype(BF16)
    h_scr[HALO_ROWS + tm:] = _norm_mod(xn_ref[0], g, shift, scale).astype(BF16)
    top = jnp.where(i > 0, 1.0, 0.0)
    bottom = jnp.where(i < pl.num_programs(1) - 1, 1.0, 0.0)
    h = h_scr[...]
    for c0 in range(0, dff, FFN_COLS):
        gate = jnp.dot(h, w_ref[:, c0:c0 + FFN_COLS], preferred_element_type=F32)
        gate = jnp.concatenate([gate[:HALO_ROWS] * top, gate[HALO_ROWS:HALO_ROWS + tm],
                                gate[HALO_ROWS + tm:] * bottom], axis=0)
        val = jnp.dot(h, w_ref[:, dff + c0:dff + c0 + FFN_COLS], preferred_element_type=F32)
        cw = 0.5 * cw_ref[:, c0:c0 + FFN_COLS]
        half = 0.5 * cb_ref[:, c0:c0 + FFN_COLS] + gate * cw[1:2]
        half = half + pltpu.roll(gate, 1, 0) * cw[0:1]
        half = half + pltpu.roll(gate, rows - 1, 0) * cw[2:3]
        act = half * (1.0 + jnp.tanh(half)) * val
        o_ref[0, :, c0:c0 + FFN_COLS] = act[HALO_ROWS:HALO_ROWS + tm].astype(o_ref.dtype)


def _ffn_up(x, g_all, mod_all, w_all, conv_w_all, conv_b_all, layer, *, shift_row, tm=1024):
    bsz, t, d = x.shape
    dff = w_all.shape[2] // 2
    hb = tm // HALO_ROWS
    last = t // HALO_ROWS - 1
    return pl.pallas_call(
        functools.partial(_ffn_up_kernel, shift_row=shift_row),
        grid=(bsz, t // tm),
        in_specs=[
            pl.BlockSpec((1, HALO_ROWS, d), lambda b, i: (b, jnp.maximum(i * hb - 1, 0), 0)),
            pl.BlockSpec((1, tm, d), lambda b, i: (b, i, 0)),
            pl.BlockSpec((1, HALO_ROWS, d), lambda b, i: (b, jnp.minimum((i + 1) * hb, last), 0)),
            pl.BlockSpec((None, 1, d), lambda b, i: (layer, 0, 0)),
            pl.BlockSpec((None, 1, 6, d), lambda b, i: (layer, b, 0, 0)),
            pl.BlockSpec((None, d, 2 * dff), lambda b, i: (layer, 0, 0), pipeline_mode=pl.Buffered(1)),
            pl.BlockSpec((None, FFN_CONV, dff), lambda b, i: (layer, 0, 0)),
            pl.BlockSpec((None, 1, dff), lambda b, i: (layer, 0, 0)),
        ],
        out_specs=pl.BlockSpec((1, tm, dff), lambda b, i: (b, i, 0)),
        out_shape=jax.ShapeDtypeStruct((bsz, t, dff), BF16),
        scratch_shapes=[pltpu.VMEM((tm + 2 * HALO_ROWS, d), BF16)],
        compiler_params=_params(("arbitrary", "arbitrary")),
        name="ffn_up_conv",
    )(x, x, x, g_all, mod_all, w_all, conv_w_all, conv_b_all)


def _ffn_down_kernel(a_ref, w_ref, x_ref, mod_ref, *rest, gate_row):
    y = jnp.dot(a_ref[0], w_ref[...], preferred_element_type=F32)
    x = x_ref[0] + mod_ref[0][gate_row:gate_row + 1] * y
    if len(rest) == 2:
        g_ref, o_ref = rest
        x = x * lax.rsqrt(jnp.mean(x * x, axis=-1, keepdims=True) + RMS_EPS) * g_ref[...]
    else:
        o_ref, = rest
    o_ref[0] = x


def _ffn_down(a, w_all, x, mod_all, layer, final_g, *, gate_row, tm=512):
    bsz, t, d = x.shape
    dff = a.shape[-1]
    extra = [] if final_g is None else [final_g]
    return pl.pallas_call(
        functools.partial(_ffn_down_kernel, gate_row=gate_row),
        grid=(bsz, t // tm),
        in_specs=[
            pl.BlockSpec((1, tm, dff), lambda b, i: (b, i, 0)),
            pl.BlockSpec((None, dff, d), lambda b, i: (layer, 0, 0)),
            pl.BlockSpec((1, tm, d), lambda b, i: (b, i, 0)),
            pl.BlockSpec((None, 1, 6, d), lambda b, i: (layer, b, 0, 0)),
        ] + [pl.BlockSpec((1, d), lambda b, i: (0, 0)) for _ in extra],
        out_specs=pl.BlockSpec((1, tm, d), lambda b, i: (b, i, 0)),
        out_shape=jax.ShapeDtypeStruct((bsz, t, d), F32),
        compiler_params=_params(("arbitrary", "arbitrary")),
        name="ffn_down",
    )(a, w_all, x, mod_all, *extra)


def kernel(x, c, ada_w, ada_b, norm_mix_g, norm_ffn_g, even_w_in, even_w_out, hgrn_lb_logits, hgrn_norm_g, na_rpb, odd_w_in, odd_w_out, pool_w, pool_scale, ffn_w_up, ffn_conv_w, ffn_conv_b, ffn_w_down, final_norm_g):
    bsz, t, d = x.shape
    depth = ada_w.shape[0]
    a_width = hgrn_lb_logits.shape[1]
    n_groups = len(C_CONFIGS)
    c_width = C_HPG * ATT_DH
    n_c = 3 * n_groups * c_width

    lb_soft = jax.nn.softmax(hgrn_lb_logits.astype(F32), axis=0)
    lower_bounds = jnp.cumsum(lb_soft, axis=0) - lb_soft[0]
    slopes = jnp.exp2(-8.0 * jnp.arange(1, n_groups * C_HPG + 1, dtype=F32) / (n_groups * C_HPG))
    slopes = slopes.reshape(n_groups, C_HPG)

    mod_all = _ada_mod(c, ada_w, ada_b).reshape(depth, bsz, 6, d)

    g_mix, g_ffn = norm_mix_g.reshape(depth, 1, d), norm_ffn_g.reshape(depth, 1, d)
    even_in, odd_in = even_w_in.astype(BF16), odd_w_in.astype(BF16)
    even_out, odd_out = even_w_out.astype(BF16), odd_w_out.astype(BF16)
    w_up, w_down = ffn_w_up.astype(BF16), ffn_w_down.astype(BF16)
    conv_b = ffn_conv_b.reshape(depth, 1, -1)
    na_tables = _na_bias_table(na_rpb, t // GRID_W)
    qkv = lambda g: tuple(((k * n_groups + g) * c_width, c_width) for k in range(3))

    for l in range(depth):
        if l % 2 == 0:
            e = l // 2
            b_width = (even_in.shape[2] - 5 * a_width) // 3
            b_cols = tuple((5 * a_width + k * b_width, b_width) for k in range(3))
            z_a, z_b = _in_proj(x, g_mix, mod_all, even_in, l, e, (((0, 5 * a_width),), b_cols),
                                (F32, BF16), tm=512)
            o_a = _hgrn2(z_a, lower_bounds[e], hgrn_norm_g[e])
            o_b = _neighbourhood_attention(z_b, na_tables, e)
            x = _out_proj([o_a, o_b], even_out, e, x, mod_all, l, gate_row=2, odd=False)
        else:
            o_i = l // 2
            acc = None
            for g, (_, dil) in enumerate(C_CONFIGS):
                if dil == 1:
                    z_g, z_d = _in_proj(x, g_mix, mod_all, odd_in, l, o_i, (qkv(g), ((n_c, odd_in.shape[2] - n_c),)),
                                        (BF16, F32), tm=DIL_TILE[dil])
                else:
                    z_g, = _in_proj(x, g_mix, mod_all, odd_in, l, o_i, (qkv(g),), (BF16,),
                                    tm=DIL_TILE[dil], perm=dil)
                acc = _dilated_group(z_g, dil, slopes[g], DIL_TILE[dil], acc, emit_lse=g < n_groups - 1)
            o_d = _multiscale_pool(z_d, pool_w[o_i], pool_scale[o_i])
            x = _out_proj([acc[0], o_d], odd_out, o_i, x, mod_all, l, gate_row=2, odd=True)
        a = _ffn_up(x, g_ffn, mod_all, w_up, ffn_conv_w, conv_b, l, shift_row=3)
        final_g = final_norm_g.reshape(1, d) if l == depth - 1 else None
        x = _ffn_down(a, w_down, x, mod_all, l, final_g, gate_row=5)
    return x
```

```python
import functools

import jax
import jax.numpy as jnp
import numpy as np
from jax import lax
from jax.experimental import pallas as pl
from jax.experimental.pallas import tpu as pltpu

F32 = jnp.float32
BF16 = jnp.bfloat16

GRID_W = 64
RMS_EPS = 1e-6
LB_FLOOR = 1e-12
MASK_VALUE = -1e30
A_DK = 128
HGRN_TILE = 128
HGRN_GROUP = 4
HGRN_LEVELS = (4, 8, 16, 32, 64)
ATT_DH = 64
ATT_SLAB = 256
ATT_HPS = ATT_SLAB // ATT_DH
NA_KR = 8
NA_KC = 16
NA_ROWS_PER_STEP = 8
C_CONFIGS = ((128, 1), (512, 4), (2048, 16))
C_HPG = 8
DIL_TILE = {1: 1024, 4: 1024, 16: 2048}
ATT_SUB = 128
ATT_HALO = 64
POOL_WINDOWS = (2, 4, 8, 16)
POOL_HALO = 8
FFN_CONV = 3
FFN_COLS = 256
HALO_ROWS = 8
LANES = 128
assert all(w & (w - 1) == 0 and w // 2 <= HALO_ROWS for w in POOL_WINDOWS)
VMEM_LIMIT = 56 * 1024 * 1024


def _params(sem):
    return pltpu.CompilerParams(dimension_semantics=sem, vmem_limit_bytes=VMEM_LIMIT)


def _silu(v):
    return v * (1.0 / (1.0 + jnp.exp(-v)))


def _norm_mod(x, g, shift, scale):
    ms = jnp.mean(x * x, axis=-1, keepdims=True)
    return x * lax.rsqrt(ms + RMS_EPS) * (g * (1.0 + scale)) + shift


def _ada_kernel(c_ref, w_ref, b_ref, o_ref):
    c = c_ref[...]
    ca = _silu(c).astype(BF16)
    o_ref[0] = jnp.dot(ca, w_ref[0].astype(BF16), preferred_element_type=F32) + b_ref[0]


def _ada_mod(c, ada_w, ada_b):
    depth, d, n = ada_w.shape
    bsz = c.shape[0]
    tn = 1024
    return pl.pallas_call(
        _ada_kernel,
        grid=(depth, n // tn),
        in_specs=[
            pl.BlockSpec((bsz, d), lambda l, j: (0, 0)),
            pl.BlockSpec((1, d, tn), lambda l, j: (l, 0, j)),
            pl.BlockSpec((1, 1, tn), lambda l, j: (l, 0, j)),
        ],
        out_specs=pl.BlockSpec((1, bsz, tn), lambda l, j: (l, 0, j)),
        out_shape=jax.ShapeDtypeStruct((depth, bsz, n), F32),
        compiler_params=_params(("arbitrary", "arbitrary")),
        name="ada_mod",
    )(c, ada_w, ada_b.reshape(depth, 1, n))


def _in_proj_kernel(x_ref, g_ref, mod_ref, *rest, shift_row, perm, col_groups):
    n_out = len(col_groups)
    n_w = sum(len(ranges) for ranges in col_groups)
    w_refs, rest = rest[:n_w], rest[n_w:]
    o_refs, h_ref = rest[:n_out], rest[n_out]
    m = mod_ref[0]
    h = _norm_mod(x_ref[0], g_ref[...], m[shift_row:shift_row + 1], m[shift_row + 1:shift_row + 2])
    tm = h.shape[0]
    if perm == 1:
        h_ref[...] = h.astype(BF16)
        parts = 1
    else:
        h32 = rest[n_out + 1]
        for j in range(h32.shape[0]):
            h32[j] = h[:, j * LANES:(j + 1) * LANES]
        parts = 2
    cs = tm // perm
    rows = tm // parts
    for p in range(parts):
        if perm > 1:
            for r in range(p * perm // parts, (p + 1) * perm // parts):
                for j in range(h32.shape[0]):
                    h_ref[r * cs:(r + 1) * cs, j * LANES:(j + 1) * LANES] = (
                        h32[j, pl.ds(r, cs, stride=perm), :].astype(BF16))
        hb = h_ref[p * rows:(p + 1) * rows]
        w_iter = iter(w_refs)
        for o_ref, ranges in zip(o_refs, col_groups):
            off = 0
            for _, width in ranges:
                o_ref[0, p * rows:(p + 1) * rows, off:off + width] = jnp.dot(
                    hb, next(w_iter)[...], preferred_element_type=F32).astype(o_ref.dtype)
                off += width


def _in_proj(x, g_all, mod_all, w_all, layer, w_layer, col_groups, out_dtypes, *, tm=1024, perm=1):
    bsz, t, d = x.shape
    widths = [sum(w for _, w in ranges) for ranges in col_groups]
    w_specs = []
    for ranges in col_groups:
        for start, width in ranges:
            assert start % width == 0
            w_specs.append(pl.BlockSpec((None, d, width), lambda b, i, blk=start // width: (w_layer, 0, blk),
                                        pipeline_mode=pl.Buffered(1)))
    return pl.pallas_call(
        functools.partial(_in_proj_kernel, shift_row=0, perm=perm, col_groups=col_groups),
        grid=(bsz, t // tm),
        in_specs=[
            pl.BlockSpec((1, tm, d), lambda b, i: (b, i, 0)),
            pl.BlockSpec((None, 1, d), lambda b, i: (layer, 0, 0)),
            pl.BlockSpec((None, 1, 6, d), lambda b, i: (layer, b, 0, 0)),
        ] + w_specs,
        out_specs=[pl.BlockSpec((1, tm, w), lambda b, i: (b, i, 0)) for w in widths],
        out_shape=[jax.ShapeDtypeStruct((bsz, t, w), dt) for w, dt in zip(widths, out_dtypes)],
        scratch_shapes=[pltpu.VMEM((tm, d), BF16)] + ([pltpu.VMEM((d // LANES, tm, LANES), F32)] if perm > 1 else []),
        compiler_params=_params(("arbitrary", "arbitrary")),
        name="in_proj",
    )(x, g_all, mod_all, *([w_all] * len(w_specs)))


def _split3(x):
    hi = x.astype(BF16)
    r = x - hi.astype(F32)
    mid = r.astype(BF16)
    lo = (r - mid.astype(F32)).astype(BF16)
    return hi, mid, lo


def _span_code(later, earlier):
    g = HGRN_GROUP
    code = jnp.where(((later // g) == (earlier // g)) & (later >= earlier), later - earlier, -1)
    for li, m in enumerate(HGRN_LEVELS):
        straddles = ((later // (2 * m)) == (earlier // (2 * m))) & ((later % (2 * m)) >= m) & ((earlier % (2 * m)) < m)
        code = jnp.where(straddles, g + li, code)
    return code


def _hgrn_kernel(q_ref, ff_ref, fb_ref, v_ref, g_ref, lb_ref, ng_ref, o_ref, acc_scr, qs_scr):
    t = q_ref.shape[1]
    tl = HGRN_TILE
    g8 = HGRN_GROUP
    n_tiles = t // tl
    log2e = 1.4426950408889634

    lb = lb_ref[...]
    lb_floor = jnp.maximum(lb, LB_FLOOR)
    one_m_lb = 1.0 - lb
    log2_one_m_lb = jnp.log2(one_m_lb)

    row = lax.broadcasted_iota(jnp.int32, (tl, tl), 0)
    col = lax.broadcasted_iota(jnp.int32, (tl, tl), 1)
    tri = jnp.where(col <= row, 1.0, 0.0).astype(BF16)
    code_f = _span_code(row, col)
    code_b = _span_code(col, row)
    r2 = lax.broadcasted_iota(jnp.int32, (2 * tl, 2 * tl), 0) < tl
    c2 = lax.broadcasted_iota(jnp.int32, (2 * tl, 2 * tl), 1) < tl
    ones_bd = jnp.where(r2 == c2, 1.0, 0.0).astype(BF16)

    def gates(fz):
        e = jnp.exp(-jnp.abs(fz))
        r = 1.0 / (1.0 + e)
        pos = fz >= 0
        sig_pos = jnp.where(pos, r, e * r)
        sig_neg = jnp.where(pos, e * r, r)
        log2_sig_neg = -(jnp.maximum(fz, 0.0) * log2e + jnp.log2(1.0 + e))
        return jnp.log2(lb_floor + one_m_lb * sig_pos), one_m_lb * sig_neg, log2_one_m_lb + log2_sig_neg

    def rot_group(x, shift):
        return pltpu.roll(x.reshape(tl // HALO_ROWS, HALO_ROWS, x.shape[1]), shift % HALO_ROWS, 1).reshape(x.shape)

    def span_ref(b, fwd, m):
        blocks = []
        for a in range(0, tl, 2 * m):
            edge = a + m - 1 if fwd else a + m
            blocks.append(jnp.broadcast_to(b[edge:edge + 1], (2 * m, b.shape[1])))
        return jnp.concatenate(blocks, axis=0)

    acc_scr[...] = jnp.zeros_like(acc_scr)

    def silu_body(i, carry):
        t0 = pl.multiple_of(i * tl, tl)
        qs_scr[pl.ds(t0, tl), :] = _silu(q_ref[0, pl.ds(t0, tl), :])
        return carry

    lax.fori_loop(0, n_tiles, silu_body, 0)

    def body(n, carry):
        st_f, st_b = carry
        tf = pl.multiple_of(n * tl, tl)
        tb = pl.multiple_of((n_tiles - 1 - n) * tl, tl)
        q_f = qs_scr[pl.ds(tf, tl), :]
        q_b = qs_scr[pl.ds(tb, tl), :]
        v_f = v_ref[0, pl.ds(tf, tl), :].astype(BF16)
        v_b = v_ref[0, pl.ds(tb, tl), :].astype(BF16)
        lf_f, kk_f, lk_f = gates(ff_ref[0, pl.ds(tf, tl), :])
        lf_b, kk_b, lk_b = gates(fb_ref[0, pl.ds(tb, tl), :])

        pieces = jnp.concatenate(_split3(lf_f) + _split3(lf_b), axis=1)
        ps = jnp.dot(tri, pieces, preferred_element_type=F32)
        b_f = ps[:, 0:A_DK] + ps[:, A_DK:2 * A_DK] + ps[:, 2 * A_DK:3 * A_DK]
        p_b = ps[:, 3 * A_DK:4 * A_DK] + ps[:, 4 * A_DK:5 * A_DK] + ps[:, 5 * A_DK:6 * A_DK]
        b_b = p_b[tl - 1:tl] - p_b + lf_b

        w_f, w_b = lk_f - b_f, lk_b - b_b
        lhs = []
        for e in range(g8):
            wf_e = w_f if e == 0 else rot_group(w_f, e)
            wb_e = w_b if e == 0 else rot_group(w_b, HALO_ROWS - e)
            pf = q_f * jnp.exp2(jnp.minimum(b_f + wf_e, 0.0))
            pb = q_b * jnp.exp2(jnp.minimum(b_b + wb_e, 0.0))
            lhs.append(jnp.concatenate([pf, pb], axis=1).astype(BF16))
        half = g8 // 2
        att = [jnp.dot(jnp.concatenate(part, axis=0), ones_bd, preferred_element_type=F32)
               for part in (lhs[:half], lhs[half:])]
        a_f = jnp.zeros((tl, tl), F32)
        a_b = jnp.zeros((tl, tl), F32)
        for e in range(g8):
            blk = att[e // half][(e % half) * tl:(e % half + 1) * tl]
            a_f = jnp.where(code_f == e, blk[:, :A_DK], a_f)
            a_b = jnp.where(code_b == e, blk[:, A_DK:], a_b)

        for li, m in enumerate(HGRN_LEVELS):
            e_f = jnp.exp2(-jnp.abs(b_f - span_ref(b_f, True, m)))
            e_b = jnp.exp2(-jnp.abs(b_b - span_ref(b_b, False, m)))
            s_f = lax.dot_general((q_f * e_f).astype(BF16), (kk_f * e_f).astype(BF16),
                                  (((1,), (1,)), ((), ())), preferred_element_type=F32)
            s_b = lax.dot_general((q_b * e_b).astype(BF16), (kk_b * e_b).astype(BF16),
                                  (((1,), (1,)), ((), ())), preferred_element_type=F32)
            a_f = jnp.where(code_f == g8 + li, s_f, a_f)
            a_b = jnp.where(code_b == g8 + li, s_b, a_b)

        def inter(q, v_bf, kk, b, edge, st):
            b_edge = b[edge:edge + 1]
            qb = (q * jnp.exp2(b)).astype(BF16)
            o = lax.dot_general(qb, st.astype(BF16), (((1,), (1,)), ((), ())), preferred_element_type=F32)
            kd = (kk * jnp.exp2(b_edge - b)).astype(BF16)
            kv = lax.dot_general(v_bf, kd, (((0,), (0,)), ((), ())), preferred_element_type=F32)
            return o, st * jnp.exp2(b_edge) + kv

        oi_f, st_f = inter(q_f, v_f, kk_f, b_f, tl - 1, st_f)
        oi_b, st_b = inter(q_b, v_b, kk_b, b_b, 0, st_b)
        acc_scr[pl.ds(tf, tl), :] += jnp.dot(a_f.astype(BF16), v_f, preferred_element_type=F32) + oi_f
        acc_scr[pl.ds(tb, tl), :] += jnp.dot(a_b.astype(BF16), v_b, preferred_element_type=F32) + oi_b
        return st_f, st_b

    zero = jnp.zeros((A_DK, A_DK), F32)
    lax.fori_loop(0, n_tiles, body, (zero, zero), unroll=8)

    ng = ng_ref[...]

    def out_body(i, carry):
        t0 = pl.multiple_of(i * tl, tl)
        o = acc_scr[pl.ds(t0, tl), :]
        o = o * lax.rsqrt(jnp.mean(o * o, axis=-1, keepdims=True) + RMS_EPS) * ng
        o_ref[0, pl.ds(t0, tl), :] = (o * _silu(g_ref[0, pl.ds(t0, tl), :])).astype(o_ref.dtype)
        return carry

    lax.fori_loop(0, n_tiles, out_body, 0)


def _hgrn2(z, lb, norm_g):
    bsz, t, w5 = z.shape
    heads = w5 // 5 // A_DK

    def col(k):
        return pl.BlockSpec((1, t, A_DK), lambda b, h, k=k: (b, 0, h + heads * k))

    return pl.pallas_call(
        _hgrn_kernel,
        grid=(bsz, heads),
        in_specs=[col(0), col(1), col(2), col(3), col(4),
                  pl.BlockSpec((1, A_DK), lambda b, h: (0, h)),
                  pl.BlockSpec((1, A_DK), lambda b, h: (0, 0))],
        out_specs=pl.BlockSpec((1, t, A_DK), lambda b, h: (b, 0, h)),
        out_shape=jax.ShapeDtypeStruct((bsz, t, heads * A_DK), BF16),
        scratch_shapes=[pltpu.VMEM((t, A_DK), F32)] * 2,
        compiler_params=_params(("arbitrary", "arbitrary")),
        name="hgrn2",
    )(z, z, z, z, z, lb.reshape(1, heads * A_DK), norm_g.reshape(1, A_DK))


def _stacked_attention(q, k, v, bias, extra_ok):
    m_rows = q.shape[0]
    heads = q.shape[1] // ATT_DH
    head_of_lane = lax.broadcasted_iota(jnp.int32, (1, q.shape[1]), 1) // ATT_DH
    zero = jnp.zeros((), q.dtype)
    q_bd = jnp.concatenate([jnp.where(head_of_lane == h, q, zero) for h in range(heads)], axis=0)
    s = lax.dot_general(q_bd, k, (((1,), (1,)), ((), ())), preferred_element_type=F32) + bias
    if extra_ok is not None:
        s = jnp.where(extra_ok, s, MASK_VALUE)
    m = jnp.max(s, axis=-1, keepdims=True)
    p = jnp.exp(s - m)
    den = jnp.sum(p, axis=-1, keepdims=True)
    o = jnp.dot(p.astype(BF16), v, preferred_element_type=F32) / den
    lse = m + jnp.log(den)
    out = o[0:m_rows]
    lse_b = jnp.broadcast_to(lse[0:m_rows], out.shape)
    for h in range(1, heads):
        sel = head_of_lane == h
        out = jnp.where(sel, o[h * m_rows:(h + 1) * m_rows], out)
        lse_b = jnp.where(sel, lse[h * m_rows:(h + 1) * m_rows], lse_b)
    return out, lse_b


def _na_kernel(q_ref, k_ref, v_ref, *rest, rows, kr):
    bias_refs, o_ref = rest[:-1], rest[-1]
    nk = kr * GRID_W
    nb = ATT_HPS * GRID_W
    scale = ATT_DH ** -0.5
    for j, bias_ref in enumerate(bias_refs):
        r = pl.program_id(1) * len(bias_refs) + j
        rs = jnp.clip(r - kr // 2, 0, rows - kr)
        k0 = pl.multiple_of(rs * GRID_W, GRID_W)
        q = q_ref[0, j * GRID_W:(j + 1) * GRID_W, :] * scale
        for s in range(q.shape[-1] // ATT_SLAB):
            sl = slice(s * ATT_SLAB, (s + 1) * ATT_SLAB)
            out, _ = _stacked_attention(q[:, sl], k_ref[0, pl.ds(k0, nk), sl], v_ref[0, pl.ds(k0, nk), sl],
                                        bias_ref[0, s * nb:(s + 1) * nb, :], None)
            o_ref[0, j * GRID_W:(j + 1) * GRID_W, sl] = out.astype(o_ref.dtype)


def _na_bias_table(rpb_all, rows):
    layers, heads = rpb_all.shape[:2]
    kr = min(NA_KR, rows)
    col = np.arange(GRID_W)
    cs = np.clip(col - NA_KC // 2, 0, GRID_W - NA_KC)
    kc = np.arange(GRID_W)
    rel = kc[None, :] - col[:, None] + NA_KC - 1
    ok = (kc[None, :] >= cs[:, None]) & (kc[None, :] < cs[:, None] + NA_KC)
    pick_col = ((rel[None] == np.arange(2 * NA_KC - 1)[:, None, None]) & ok[None]).astype(np.float32)
    di = np.arange(kr)[None, :] - np.arange(kr)[:, None] + NA_KR - 1
    pick_row = (di[:, :, None] == np.arange(2 * NA_KR - 1)[None, None, :]).astype(np.float32)
    table = jnp.einsum('lhdr,vid,rck->lvhcik', rpb_all.astype(F32), pick_row, pick_col,
                       precision=lax.Precision.HIGHEST)
    table = jnp.where(ok[None, None, None, :, None, :], table, MASK_VALUE)
    return table.reshape(layers, kr, heads * GRID_W, kr * GRID_W)


def _neighbourhood_attention(zb, table, layer):
    bsz, t, w3 = zb.shape
    width = w3 // 3
    heads = width // ATT_DH
    rows = t // GRID_W
    kr = min(NA_KR, rows)
    rps = NA_ROWS_PER_STEP

    def bias_spec(j):
        def index(b, i):
            r = i * rps + j
            return (layer, r - jnp.clip(r - kr // 2, 0, rows - kr), 0, 0)
        return pl.BlockSpec((None, 1, heads * GRID_W, kr * GRID_W), index)

    return pl.pallas_call(
        functools.partial(_na_kernel, rows=rows, kr=kr),
        grid=(bsz, rows // rps),
        in_specs=[
            pl.BlockSpec((1, rps * GRID_W, width), lambda b, i: (b, i, 0)),
            pl.BlockSpec((1, t, width), lambda b, i: (b, 0, 1)),
            pl.BlockSpec((1, t, width), lambda b, i: (b, 0, 2)),
        ] + [bias_spec(j) for j in range(rps)],
        out_specs=pl.BlockSpec((1, rps * GRID_W, width), lambda b, i: (b, i, 0)),
        out_shape=jax.ShapeDtypeStruct((bsz, t, width), BF16),
        compiler_params=_params(("arbitrary", "arbitrary")),
        name="neighbourhood_attention",
    )(zb, zb, zb, *([table] * rps))


def _dil_kernel(q_ref, kp_ref, k_ref, kn_ref, vp_ref, v_ref, vn_ref, bias_ref, *rest, has_prev, emit_lse):
    n_in = 2 if has_prev else 0
    n_out = 2 if emit_lse else 1
    prev_refs, out_refs = rest[:n_in], rest[n_in:n_in + n_out]
    k_scr, v_scr = rest[n_in + n_out:]
    i = pl.program_id(1)
    last_tile = pl.num_programs(1) - 1
    dil, cs = q_ref.shape[2], q_ref.shape[3]
    width = q_ref.shape[4]
    scale = ATT_DH ** -0.5
    nk = ATT_SUB + 2 * ATT_HALO
    nb = ATT_HPS * ATT_SUB
    n_sub = cs // ATT_SUB
    for r in range(dil):
        k_scr[0:ATT_HALO] = kp_ref[0, 0, r]
        k_scr[ATT_HALO:ATT_HALO + cs] = k_ref[0, 0, r]
        k_scr[ATT_HALO + cs:] = kn_ref[0, 0, r]
        v_scr[0:ATT_HALO] = vp_ref[0, 0, r]
        v_scr[ATT_HALO:ATT_HALO + cs] = v_ref[0, 0, r]
        v_scr[ATT_HALO + cs:] = vn_ref[0, 0, r]
        for sb in range(n_sub):
            r0 = sb * ATT_SUB
            edge = jnp.int32(0)
            if sb == 0:
                edge = edge + jnp.where(i == 0, 1, 0)
            if sb == n_sub - 1:
                edge = edge + jnp.where(i == last_tile, 2, 0)
            q = q_ref[0, 0, r, r0:r0 + ATT_SUB, :] * scale
            if dil == 1:
                rows_out = pl.ds(r0, ATT_SUB)
            else:
                rows_out = pl.ds(r0 * dil + r, ATT_SUB, stride=dil)
            for s in range(width // ATT_SLAB):
                sl = slice(s * ATT_SLAB, (s + 1) * ATT_SLAB)
                out, lse = _stacked_attention(q[:, sl], k_scr[r0:r0 + nk, sl], v_scr[r0:r0 + nk, sl],
                                              bias_ref[edge, s * nb:(s + 1) * nb, :], None)
                for j in range(ATT_SLAB // LANES):
                    ch = s * (ATT_SLAB // LANES) + j
                    o_j, l_j = out[:, j * LANES:(j + 1) * LANES], lse[:, j * LANES:(j + 1) * LANES]
                    if has_prev:
                        o_p, l_p = prev_refs[0][0, ch, rows_out, :], prev_refs[1][0, ch, rows_out, :]
                        m = jnp.maximum(l_p, l_j)
                        e_p, e_j = jnp.exp(l_p - m), jnp.exp(l_j - m)
                        den = e_p + e_j
                        o_j = (e_p * o_p + e_j * o_j) / den
                        l_j = m + jnp.log(den)
                    out_refs[0][0, ch, rows_out, :] = o_j
                    if emit_lse:
                        out_refs[1][0, ch, rows_out, :] = l_j


def _alibi_band(slopes_g, dil):
    nk = ATT_SUB + 2 * ATT_HALO
    rel = np.arange(nk)[None, :] - ATT_HALO - np.arange(ATT_SUB)[:, None]
    band = np.abs(rel) <= ATT_HALO
    dist = jnp.asarray(np.abs(rel) * dil, F32)
    bias = jnp.where(band[None], -slopes_g[:, None, None] * dist[None], MASK_VALUE)
    bias = bias.reshape(slopes_g.shape[0] * ATT_SUB, nk)
    kcol = np.arange(nk)
    before, after = kcol < ATT_HALO, kcol >= ATT_HALO + ATT_SUB
    cut = lambda cols: jnp.where(cols[None, :], MASK_VALUE, bias)
    return jnp.stack([bias, cut(before), cut(after), cut(before | after)])


def _dilated_group(zp, dil, slopes_g, tm, prev, emit_lse):
    bsz, t, f = zp.shape
    width = f // 3
    cs = tm // dil
    nt = t // tm
    zv = zp.reshape(bsz, nt, dil, cs, f)
    bias = _alibi_band(slopes_g, dil)
    tail = cs // ATT_HALO - 1
    prev = () if prev is None else tuple(prev)

    def main(off):
        return pl.BlockSpec((1, 1, dil, cs, width), lambda b, i: (b, i, 0, 0, off))

    def before(off):
        return pl.BlockSpec((1, 1, dil, ATT_HALO, width), lambda b, i: (b, jnp.maximum(i - 1, 0), 0, tail, off))

    def after(off):
        return pl.BlockSpec((1, 1, dil, ATT_HALO, width), lambda b, i: (b, jnp.minimum(i + 1, nt - 1), 0, 0, off))

    tok_spec = pl.BlockSpec((1, width // LANES, tm, LANES), lambda b, i: (b, 0, i, 0))
    n_out = 2 if emit_lse else 1
    return pl.pallas_call(
        functools.partial(_dil_kernel, has_prev=bool(prev), emit_lse=emit_lse),
        grid=(bsz, nt),
        in_specs=[main(0), before(1), main(1), after(1), before(2), main(2), after(2),
                  pl.BlockSpec(bias.shape, lambda b, i: (0, 0, 0))] + [tok_spec] * len(prev),
        out_specs=[tok_spec] * n_out,
        out_shape=[jax.ShapeDtypeStruct((bsz, width // LANES, t, LANES), F32)] * n_out,
        scratch_shapes=[pltpu.VMEM((cs + 2 * ATT_HALO, width), BF16)] * 2,
        compiler_params=_params(("arbitrary", "arbitrary")),
        name=f"dilated_attention_d{dil}",
    )(zv, zv, zv, zv, zv, zv, zv, bias, *prev)


def _pool_kernel(up_ref, u_ref, un_ref, w_ref, sc_ref, o_ref, scr, *, seq_len):
    i = pl.program_id(1)
    tm = u_ref.shape[1]
    gc = w_ref.shape[1]
    scr[0:HALO_ROWS] = jnp.where(i > 0, up_ref[0], 0.0)
    scr[HALO_ROWS:HALO_ROWS + tm] = u_ref[0]
    scr[HALO_ROWS + tm:] = jnp.where(i < pl.num_programs(1) - 1, un_ref[0], 0.0)
    tpos = i * tm + lax.broadcasted_iota(jnp.int32, (tm, 1), 0)
    for g, w in enumerate(POOL_WINDOWS):
        sl = slice(g * gc, (g + 1) * gc)
        run = scr[:, sl]
        span = 1
        while span < w:
            run = run + pltpu.roll(run, span, 0)
            span *= 2
        ahead = w // 2 - 1
        if ahead:
            run = pltpu.roll(run, run.shape[0] - ahead, 0)
        tot = run[HALO_ROWS:HALO_ROWS + tm]
        lo = jnp.clip(tpos - w // 2, 0, seq_len)
        hi = jnp.clip(tpos + w - w // 2, 0, seq_len)
        mean = tot / (hi - lo).astype(F32)
        diff = mean - scr[HALO_ROWS:HALO_ROWS + tm, sl]
        y = jnp.dot(diff.astype(BF16), w_ref[g], preferred_element_type=F32)
        o_ref[0, :, sl] = (y * sc_ref[:, sl]).astype(o_ref.dtype)


def _multiscale_pool(u, w_groups, scale, tm=512):
    bsz, t, ch = u.shape
    hb = tm // HALO_ROWS
    last = t // HALO_ROWS - 1
    return pl.pallas_call(
        functools.partial(_pool_kernel, seq_len=t),
        grid=(bsz, t // tm),
        in_specs=[
            pl.BlockSpec((1, HALO_ROWS, ch), lambda b, i: (b, jnp.maximum(i * hb - 1, 0), 0)),
            pl.BlockSpec((1, tm, ch), lambda b, i: (b, i, 0)),
            pl.BlockSpec((1, HALO_ROWS, ch), lambda b, i: (b, jnp.minimum((i + 1) * hb, last), 0)),
            pl.BlockSpec(w_groups.shape, lambda b, i: (0, 0, 0)),
            pl.BlockSpec((1, ch), lambda b, i: (0, 0)),
        ],
        out_specs=pl.BlockSpec((1, tm, ch), lambda b, i: (b, i, 0)),
        out_shape=jax.ShapeDtypeStruct((bsz, t, ch), BF16),
        scratch_shapes=[pltpu.VMEM((tm + 2 * HALO_ROWS, ch), F32)],
        compiler_params=_params(("arbitrary", "arbitrary")),
        name="multiscale_pool",
    )(u, u, u, w_groups.astype(BF16), scale.reshape(1, ch))


def _out_even_kernel(oa_ref, ob_ref, wa_ref, wb_ref, x_ref, mod_ref, o_ref, *, gate_row):
    mixed = jnp.dot(oa_ref[0], wa_ref[...], preferred_element_type=F32)
    mixed = mixed + jnp.dot(ob_ref[0], wb_ref[...], preferred_element_type=F32)
    gate = mod_ref[0][gate_row:gate_row + 1]
    o_ref[0] = x_ref[0] + gate * mixed


def _out_odd_kernel(oc_ref, od_ref, wa_ref, wb_ref, x_ref, mod_ref, o_ref, *, gate_row):
    oc = jnp.concatenate([oc_ref[0, j] for j in range(oc_ref.shape[1])], axis=1)
    mixed = jnp.dot(oc.astype(BF16), wa_ref[...], preferred_element_type=F32)
    mixed = mixed + jnp.dot(od_ref[0], wb_ref[...], preferred_element_type=F32)
    gate = mod_ref[0][gate_row:gate_row + 1]
    o_ref[0] = x_ref[0] + gate * mixed


def _out_proj(parts, w_all, w_layer, x, mod_all, layer, *, gate_row, odd, tm=1024):
    bsz, t, d = x.shape
    half = w_all.shape[1] // 2
    tok = lambda width: pl.BlockSpec((1, tm, width), lambda b, i: (b, i, 0))
    chunked = pl.BlockSpec((1, half // LANES, tm, LANES), lambda b, i: (b, 0, i, 0))
    kern = _out_odd_kernel if odd else _out_even_kernel
    part_specs = [chunked if p.ndim == 4 else tok(half) for p in parts]
    return pl.pallas_call(
        functools.partial(kern, gate_row=gate_row),
        grid=(bsz, t // tm),
        in_specs=part_specs + [
            pl.BlockSpec((None, half, d), lambda b, i: (w_layer, 0, 0)),
            pl.BlockSpec((None, half, d), lambda b, i: (w_layer, 1, 0)),
            tok(d),
            pl.BlockSpec((None, 1, 6, d), lambda b, i: (layer, b, 0, 0)),
        ],
        out_specs=tok(d),
        out_shape=jax.ShapeDtypeStruct((bsz, t, d), F32),
        compiler_params=_params(("arbitrary", "arbitrary")),
        name="out_proj_odd" if odd else "out_proj_even",
    )(*parts, w_all, w_all, x, mod_all)


def _ffn_up_kernel(xp_ref, x_ref, xn_ref, g_ref, mod_ref, w_ref, cw_ref, cb_ref, o_ref, h_scr, *, shift_row):
    i = pl.program_id(1)
    tm = x_ref.shape[1]
    dff = o_ref.shape[2]
    rows = tm + 2 * HALO_ROWS
    m = mod_ref[0]
    shift, scale = m[shift_row:shift_row + 1], m[shift_row + 1:shift_row + 2]
    g = g_ref[...]
    h_scr[0:HALO_ROWS] = _norm_mod(xp_ref[0], g, shift, scale).astype(BF16)
    h_scr[HALO_ROWS:HALO_ROWS + tm] = _norm_mod(x_ref[0], g, shift, scale).astype(BF16)
    h_scr[HALO_ROWS + tm:] = _norm_mod(xn_ref[0], g, shift, scale).astype(BF16)
    top = jnp.where(i > 0, 1.0, 0.0)
    bottom = jnp.where(i < pl.num_programs(1) - 1, 1.0, 0.0)
    h = h_scr[...]
    for c0 in range(0, dff, FFN_COLS):
        gate = jnp.dot(h, w_ref[:, c0:c0 + FFN_COLS], preferred_element_type=F32)
        gate = jnp.concatenate([gate[:HALO_ROWS] * top, gate[HALO_ROWS:HALO_ROWS + tm],
                                gate[HALO_ROWS + tm:] * bottom], axis=0)
        val = jnp.dot(h, w_ref[:, dff + c0:dff + c0 + FFN_COLS], preferred_element_type=F32)
        cw = 0.5 * cw_ref[:, c0:c0 + FFN_COLS]
        half = 0.5 * cb_ref[:, c0:c0 + FFN_COLS] + gate * cw[1:2]
        half = half + pltpu.roll(gate, 1, 0) * cw[0:1]
        half = half + pltpu.roll(gate, rows - 1, 0) * cw[2:3]
        act = half * (1.0 + jnp.tanh(half)) * val
        o_ref[0, :, c0:c0 + FFN_COLS] = act[HALO_ROWS:HALO_ROWS + tm].astype(o_ref.dtype)


def _ffn_up(x, g_all, mod_all, w_all, conv_w_all, conv_b_all, layer, *, shift_row, tm=1024):
    bsz, t, d = x.shape
    dff = w_all.shape[2] // 2
    hb = tm // HALO_ROWS
    last = t // HALO_ROWS - 1
    return pl.pallas_call(
        functools.partial(_ffn_up_kernel, shift_row=shift_row),
        grid=(bsz, t // tm),
        in_specs=[
            pl.BlockSpec((1, HALO_ROWS, d), lambda b, i: (b, jnp.maximum(i * hb - 1, 0), 0)),
            pl.BlockSpec((1, tm, d), lambda b, i: (b, i, 0)),
            pl.BlockSpec((1, HALO_ROWS, d), lambda b, i: (b, jnp.minimum((i + 1) * hb, last), 0)),
            pl.BlockSpec((None, 1, d), lambda b, i: (layer, 0, 0)),
            pl.BlockSpec((None, 1, 6, d), lambda b, i: (layer, b, 0, 0)),
            pl.BlockSpec((None, d, 2 * dff), lambda b, i: (layer, 0, 0), pipeline_mode=pl.Buffered(1)),
            pl.BlockSpec((None, FFN_CONV, dff), lambda b, i: (layer, 0, 0)),
            pl.BlockSpec((None, 1, dff), lambda b, i: (layer, 0, 0)),
        ],
        out_specs=pl.BlockSpec((1, tm, dff), lambda b, i: (b, i, 0)),
        out_shape=jax.ShapeDtypeStruct((bsz, t, dff), BF16),
        scratch_shapes=[pltpu.VMEM((tm + 2 * HALO_ROWS, d), BF16)],
        compiler_params=_params(("arbitrary", "arbitrary")),
        name="ffn_up_conv",
    )(x, x, x, g_all, mod_all, w_all, conv_w_all, conv_b_all)


def _ffn_down_kernel(a_ref, w_ref, x_ref, mod_ref, *rest, gate_row):
    y = jnp.dot(a_ref[0], w_ref[...], preferred_element_type=F32)
    x = x_ref[0] + mod_ref[0][gate_row:gate_row + 1] * y
    if len(rest) == 2:
        g_ref, o_ref = rest
        x = x * lax.rsqrt(jnp.mean(x * x, axis=-1, keepdims=True) + RMS_EPS) * g_ref[...]
    else:
        o_ref, = rest
    o_ref[0] = x


def _ffn_down(a, w_all, x, mod_all, layer, final_g, *, gate_row, tm=512):
    bsz, t, d = x.shape
    dff = a.shape[-1]
    extra = [] if final_g is None else [final_g]
    return pl.pallas_call(
        functools.partial(_ffn_down_kernel, gate_row=gate_row),
        grid=(bsz, t // tm),
        in_specs=[
            pl.BlockSpec((1, tm, dff), lambda b, i: (b, i, 0)),
            pl.BlockSpec((None, dff, d), lambda b, i: (layer, 0, 0)),
            pl.BlockSpec((1, tm, d), lambda b, i: (b, i, 0)),
            pl.BlockSpec((None, 1, 6, d), lambda b, i: (layer, b, 0, 0)),
        ] + [pl.BlockSpec((1, d), lambda b, i: (0, 0)) for _ in extra],
        out_specs=pl.BlockSpec((1, tm, d), lambda b, i: (b, i, 0)),
        out_shape=jax.ShapeDtypeStruct((bsz, t, d), F32),
        compiler_params=_params(("arbitrary", "arbitrary")),
        name="ffn_down",
    )(a, w_all, x, mod_all, *extra)


def kernel(x, c, ada_w, ada_b, norm_mix_g, norm_ffn_g, even_w_in, even_w_out, hgrn_lb_logits, hgrn_norm_g, na_rpb, odd_w_in, odd_w_out, pool_w, pool_scale, ffn_w_up, ffn_conv_w, ffn_conv_b, ffn_w_down, final_norm_g):
    bsz, t, d = x.shape
    depth = ada_w.shape[0]
    a_width = hgrn_lb_logits.shape[1]
    n_groups = len(C_CONFIGS)
    c_width = C_HPG * ATT_DH
    n_c = 3 * n_groups * c_width

    lb_soft = jax.nn.softmax(hgrn_lb_logits.astype(F32), axis=0)
    lower_bounds = jnp.cumsum(lb_soft, axis=0) - lb_soft[0]
    slopes = jnp.exp2(-8.0 * jnp.arange(1, n_groups * C_HPG + 1, dtype=F32) / (n_groups * C_HPG))
    slopes = slopes.reshape(n_groups, C_HPG)

    mod_all = _ada_mod(c, ada_w, ada_b).reshape(depth, bsz, 6, d)

    g_mix, g_ffn = norm_mix_g.reshape(depth, 1, d), norm_ffn_g.reshape(depth, 1, d)
    even_in, odd_in = even_w_in.astype(BF16), odd_w_in.astype(BF16)
    even_out, odd_out = even_w_out.astype(BF16), odd_w_out.astype(BF16)
    w_up, w_down = ffn_w_up.astype(BF16), ffn_w_down.astype(BF16)
    conv_b = ffn_conv_b.reshape(depth, 1, -1)
    na_tables = _na_bias_table(na_rpb, t // GRID_W)
    qkv = lambda g: tuple(((k * n_groups + g) * c_width, c_width) for k in range(3))

    for l in range(depth):
        if l % 2 == 0:
            e = l // 2
            b_width = (even_in.shape[2] - 5 * a_width) // 3
            b_cols = tuple((5 * a_width + k * b_width, b_width) for k in range(3))
            z_a, z_b = _in_proj(x, g_mix, mod_all, even_in, l, e, (((0, 5 * a_width),), b_cols),
                                (F32, BF16), tm=512)
            o_a = _hgrn2(z_a, lower_bounds[e], hgrn_norm_g[e])
            o_b = _neighbourhood_attention(z_b, na_tables, e)
            x = _out_proj([o_a, o_b], even_out, e, x, mod_all, l, gate_row=2, odd=False)
        else:
            o_i = l // 2
            acc = None
            for g in sorted(range(n_groups), key=lambda k: -C_CONFIGS[k][1]):
                dil = C_CONFIGS[g][1]
                if dil == 1:
                    z_g, z_d = _in_proj(x, g_mix, mod_all, odd_in, l, o_i, (qkv(g), ((n_c, odd_in.shape[2] - n_c),)),
                                        (BF16, F32), tm=DIL_TILE[dil])
                else:
                    z_g, = _in_proj(x, g_mix, mod_all, odd_in, l, o_i, (qkv(g),), (BF16,),
                                    tm=DIL_TILE[dil], perm=dil)
                acc = _dilated_group(z_g, dil, slopes[g], DIL_TILE[dil], acc, emit_lse=dil > 1)
            o_d = _multiscale_pool(z_d, pool_w[o_i], pool_scale[o_i])
            x = _out_proj([acc[0], o_d], odd_out, o_i, x, mod_all, l, gate_row=2, odd=True)
        a = _ffn_up(x, g_ffn, mod_all, w_up, ffn_conv_w, conv_b, l, shift_row=3)
        final_g = final_norm_g.reshape(1, d) if l == depth - 1 else None
        x = _ffn_down(a, w_down, x, mod_all, l, final_g, gate_row=5)
    return x
```

```python
import functools

import jax
import jax.numpy as jnp
import numpy as np
from jax import lax
from jax.experimental import pallas as pl
from jax.experimental.pallas import tpu as pltpu

F32 = jnp.float32
BF16 = jnp.bfloat16

GRID_W = 64
RMS_EPS = 1e-6
LB_FLOOR = 1e-12
MASK_VALUE = -1e30
A_DK = 128
HGRN_TILE = 128
HGRN_GROUP = 4
HGRN_LEVELS = (4, 8, 16, 32, 64)
ATT_DH = 64
ATT_SLAB = 256
ATT_HPS = ATT_SLAB // ATT_DH
NA_KR = 8
NA_KC = 16
NA_ROWS_PER_STEP = 8
C_CONFIGS = ((128, 1), (512, 4), (2048, 16))
C_HPG = 8
DIL_TILE = {1: 1024, 4: 1024, 16: 2048}
ATT_SUB = 128
ATT_HALO = 64
POOL_WINDOWS = (2, 4, 8, 16)
POOL_HALO = 8
FFN_CONV = 3
FFN_COLS = 256
HALO_ROWS = 8
LANES = 128
assert all(w & (w - 1) == 0 and w // 2 <= HALO_ROWS for w in POOL_WINDOWS)
VMEM_LIMIT = 56 * 1024 * 1024


def _params(sem):
    return pltpu.CompilerParams(dimension_semantics=sem, vmem_limit_bytes=VMEM_LIMIT)


def _silu(v):
    return v * (1.0 / (1.0 + jnp.exp(-v)))


def _norm_mod(x, g, shift, scale):
    ms = jnp.mean(x * x, axis=-1, keepdims=True)
    return x * lax.rsqrt(ms + RMS_EPS) * (g * (1.0 + scale)) + shift


def _ada_kernel(c_ref, w_ref, b_ref, o_ref):
    c = c_ref[...]
    ca = _silu(c).astype(BF16)
    o_ref[0] = jnp.dot(ca, w_ref[0].astype(BF16), preferred_element_type=F32) + b_ref[0]


def _ada_mod(c, ada_w, ada_b):
    depth, d, n = ada_w.shape
    bsz = c.shape[0]
    tn = 1024
    return pl.pallas_call(
        _ada_kernel,
        grid=(depth, n // tn),
        in_specs=[
            pl.BlockSpec((bsz, d), lambda l, j: (0, 0)),
            pl.BlockSpec((1, d, tn), lambda l, j: (l, 0, j)),
            pl.BlockSpec((1, 1, tn), lambda l, j: (l, 0, j)),
        ],
        out_specs=pl.BlockSpec((1, bsz, tn), lambda l, j: (l, 0, j)),
        out_shape=jax.ShapeDtypeStruct((depth, bsz, n), F32),
        compiler_params=_params(("arbitrary", "arbitrary")),
        name="ada_mod",
    )(c, ada_w, ada_b.reshape(depth, 1, n))


def _in_proj_kernel(x_ref, g_ref, mod_ref, *rest, shift_row, perm, col_groups):
    n_out = len(col_groups)
    n_w = sum(len(ranges) for ranges in col_groups)
    w_refs, rest = rest[:n_w], rest[n_w:]
    o_refs, h_ref = rest[:n_out], rest[n_out]
    m = mod_ref[0]
    h = _norm_mod(x_ref[0], g_ref[...], m[shift_row:shift_row + 1], m[shift_row + 1:shift_row + 2])
    tm = h.shape[0]
    if perm == 1:
        h_ref[...] = h.astype(BF16)
        parts = 1
    else:
        h32 = rest[n_out + 1]
        for j in range(h32.shape[0]):
            h32[j] = h[:, j * LANES:(j + 1) * LANES]
        parts = 2
    cs = tm // perm
    rows = tm // parts
    for p in range(parts):
        if perm > 1:
            for r in range(p * perm // parts, (p + 1) * perm // parts):
                for j in range(h32.shape[0]):
                    h_ref[r * cs:(r + 1) * cs, j * LANES:(j + 1) * LANES] = (
                        h32[j, pl.ds(r, cs, stride=perm), :].astype(BF16))
        hb = h_ref[p * rows:(p + 1) * rows]
        w_iter = iter(w_refs)
        for o_ref, ranges in zip(o_refs, col_groups):
            off = 0
            for _, width in ranges:
                o_ref[0, p * rows:(p + 1) * rows, off:off + width] = jnp.dot(
                    hb, next(w_iter)[...], preferred_element_type=F32).astype(o_ref.dtype)
                off += width


def _in_proj(x, g_all, mod_all, w_all, layer, w_layer, col_groups, out_dtypes, *, tm=1024, perm=1):
    bsz, t, d = x.shape
    widths = [sum(w for _, w in ranges) for ranges in col_groups]
    w_specs = []
    for ranges in col_groups:
        for start, width in ranges:
            assert start % width == 0
            w_specs.append(pl.BlockSpec((None, d, width), lambda b, i, blk=start // width: (w_layer, 0, blk),
                                        pipeline_mode=pl.Buffered(1)))
    return pl.pallas_call(
        functools.partial(_in_proj_kernel, shift_row=0, perm=perm, col_groups=col_groups),
        grid=(bsz, t // tm),
        in_specs=[
            pl.BlockSpec((1, tm, d), lambda b, i: (b, i, 0)),
            pl.BlockSpec((None, 1, d), lambda b, i: (layer, 0, 0)),
            pl.BlockSpec((None, 1, 6, d), lambda b, i: (layer, b, 0, 0)),
        ] + w_specs,
        out_specs=[pl.BlockSpec((1, tm, w), lambda b, i: (b, i, 0)) for w in widths],
        out_shape=[jax.ShapeDtypeStruct((bsz, t, w), dt) for w, dt in zip(widths, out_dtypes)],
        scratch_shapes=[pltpu.VMEM((tm, d), BF16)] + ([pltpu.VMEM((d // LANES, tm, LANES), F32)] if perm > 1 else []),
        compiler_params=_params(("arbitrary", "arbitrary")),
        name="in_proj",
    )(x, g_all, mod_all, *([w_all] * len(w_specs)))


def _split3(x):
    hi = x.astype(BF16)
    r = x - hi.astype(F32)
    mid = r.astype(BF16)
    lo = (r - mid.astype(F32)).astype(BF16)
    return hi, mid, lo


def _span_code(later, earlier):
    g = HGRN_GROUP
    code = jnp.where(((later // g) == (earlier // g)) & (later >= earlier), later - earlier, -1)
    for li, m in enumerate(HGRN_LEVELS):
        straddles = ((later // (2 * m)) == (earlier // (2 * m))) & ((later % (2 * m)) >= m) & ((earlier % (2 * m)) < m)
        code = jnp.where(straddles, g + li, code)
    return code


def _hgrn_kernel(q_ref, ff_ref, fb_ref, v_ref, g_ref, lb_ref, ng_ref, o_ref, acc_scr, qs_scr):
    t = q_ref.shape[1]
    tl = HGRN_TILE
    g8 = HGRN_GROUP
    n_tiles = t // tl
    log2e = 1.4426950408889634

    lb = lb_ref[...]
    lb_floor = jnp.maximum(lb, LB_FLOOR)
    one_m_lb = 1.0 - lb
    log2_one_m_lb = jnp.log2(one_m_lb)

    row = lax.broadcasted_iota(jnp.int32, (tl, tl), 0)
    col = lax.broadcasted_iota(jnp.int32, (tl, tl), 1)
    tri = jnp.where(col <= row, 1.0, 0.0).astype(BF16)
    code_f = _span_code(row, col)
    code_b = _span_code(col, row)
    r2 = lax.broadcasted_iota(jnp.int32, (2 * tl, 2 * tl), 0) < tl
    c2 = lax.broadcasted_iota(jnp.int32, (2 * tl, 2 * tl), 1) < tl
    ones_bd = jnp.where(r2 == c2, 1.0, 0.0).astype(BF16)

    def gates(fz):
        e = jnp.exp(-jnp.abs(fz))
        r = 1.0 / (1.0 + e)
        pos = fz >= 0
        sig_pos = jnp.where(pos, r, e * r)
        sig_neg = jnp.where(pos, e * r, r)
        log2_sig_neg = -(jnp.maximum(fz, 0.0) * log2e + jnp.log2(1.0 + e))
        return jnp.log2(lb_floor + one_m_lb * sig_pos), one_m_lb * sig_neg, log2_one_m_lb + log2_sig_neg

    def rot_group(x, shift):
        return pltpu.roll(x.reshape(tl // HALO_ROWS, HALO_ROWS, x.shape[1]), shift % HALO_ROWS, 1).reshape(x.shape)

    def span_ref(b, fwd, m):
        blocks = []
        for a in range(0, tl, 2 * m):
            edge = a + m - 1 if fwd else a + m
            blocks.append(jnp.broadcast_to(b[edge:edge + 1], (2 * m, b.shape[1])))
        return jnp.concatenate(blocks, axis=0)

    acc_scr[...] = jnp.zeros_like(acc_scr)

    def silu_body(i, carry):
        t0 = pl.multiple_of(i * tl, tl)
        qs_scr[pl.ds(t0, tl), :] = _silu(q_ref[0, pl.ds(t0, tl), :])
        return carry

    lax.fori_loop(0, n_tiles, silu_body, 0, unroll=4)

    def body(n, carry):
        st_f, st_b = carry
        tf = pl.multiple_of(n * tl, tl)
        tb = pl.multiple_of((n_tiles - 1 - n) * tl, tl)
        q_f = qs_scr[pl.ds(tf, tl), :]
        q_b = qs_scr[pl.ds(tb, tl), :]
        v_f = v_ref[0, pl.ds(tf, tl), :].astype(BF16)
        v_b = v_ref[0, pl.ds(tb, tl), :].astype(BF16)
        lf_f, kk_f, lk_f = gates(ff_ref[0, pl.ds(tf, tl), :])
        lf_b, kk_b, lk_b = gates(fb_ref[0, pl.ds(tb, tl), :])

        pieces = jnp.concatenate(_split3(lf_f) + _split3(lf_b), axis=1)
        ps = jnp.dot(tri, pieces, preferred_element_type=F32)
        b_f = ps[:, 0:A_DK] + ps[:, A_DK:2 * A_DK] + ps[:, 2 * A_DK:3 * A_DK]
        p_b = ps[:, 3 * A_DK:4 * A_DK] + ps[:, 4 * A_DK:5 * A_DK] + ps[:, 5 * A_DK:6 * A_DK]
        b_b = p_b[tl - 1:tl] - p_b + lf_b

        w_f, w_b = lk_f - b_f, lk_b - b_b
        lhs = []
        for e in range(g8):
            wf_e = w_f if e == 0 else rot_group(w_f, e)
            wb_e = w_b if e == 0 else rot_group(w_b, HALO_ROWS - e)
            pf = q_f * jnp.exp2(jnp.minimum(b_f + wf_e, 0.0))
            pb = q_b * jnp.exp2(jnp.minimum(b_b + wb_e, 0.0))
            lhs.append(jnp.concatenate([pf, pb], axis=1).astype(BF16))
        half = g8 // 2
        att = [jnp.dot(jnp.concatenate(part, axis=0), ones_bd, preferred_element_type=F32)
               for part in (lhs[:half], lhs[half:])]
        a_f = jnp.zeros((tl, tl), F32)
        a_b = jnp.zeros((tl, tl), F32)
        for e in range(g8):
            blk = att[e // half][(e % half) * tl:(e % half + 1) * tl]
            a_f = jnp.where(code_f == e, blk[:, :A_DK], a_f)
            a_b = jnp.where(code_b == e, blk[:, A_DK:], a_b)

        for li, m in enumerate(HGRN_LEVELS):
            e_f = jnp.exp2(-jnp.abs(b_f - span_ref(b_f, True, m)))
            e_b = jnp.exp2(-jnp.abs(b_b - span_ref(b_b, False, m)))
            s_f = lax.dot_general((q_f * e_f).astype(BF16), (kk_f * e_f).astype(BF16),
                                  (((1,), (1,)), ((), ())), preferred_element_type=F32)
            s_b = lax.dot_general((q_b * e_b).astype(BF16), (kk_b * e_b).astype(BF16),
                                  (((1,), (1,)), ((), ())), preferred_element_type=F32)
            a_f = jnp.where(code_f == g8 + li, s_f, a_f)
            a_b = jnp.where(code_b == g8 + li, s_b, a_b)

        def inter(q, v_bf, kk, b, edge, st):
            b_edge = b[edge:edge + 1]
            qb = (q * jnp.exp2(b)).astype(BF16)
            o = lax.dot_general(qb, st.astype(BF16), (((1,), (1,)), ((), ())), preferred_element_type=F32)
            kd = (kk * jnp.exp2(b_edge - b)).astype(BF16)
            kv = lax.dot_general(v_bf, kd, (((0,), (0,)), ((), ())), preferred_element_type=F32)
            return o, st * jnp.exp2(b_edge) + kv

        oi_f, st_f = inter(q_f, v_f, kk_f, b_f, tl - 1, st_f)
        oi_b, st_b = inter(q_b, v_b, kk_b, b_b, 0, st_b)
        acc_scr[pl.ds(tf, tl), :] += jnp.dot(a_f.astype(BF16), v_f, preferred_element_type=F32) + oi_f
        acc_scr[pl.ds(tb, tl), :] += jnp.dot(a_b.astype(BF16), v_b, preferred_element_type=F32) + oi_b
        return st_f, st_b

    zero = jnp.zeros((A_DK, A_DK), F32)
    lax.fori_loop(0, n_tiles, body, (zero, zero), unroll=8)

    ng = ng_ref[...]

    def out_body(i, carry):
        t0 = pl.multiple_of(i * tl, tl)
        o = acc_scr[pl.ds(t0, tl), :]
        o = o * lax.rsqrt(jnp.mean(o * o, axis=-1, keepdims=True) + RMS_EPS) * ng
        o_ref[0, pl.ds(t0, tl), :] = (o * _silu(g_ref[0, pl.ds(t0, tl), :])).astype(o_ref.dtype)
        return carry

    lax.fori_loop(0, n_tiles, out_body, 0, unroll=4)


def _hgrn2(z, lb, norm_g):
    bsz, t, w5 = z.shape
    heads = w5 // 5 // A_DK

    def col(k):
        return pl.BlockSpec((1, t, A_DK), lambda b, h, k=k: (b, 0, h + heads * k))

    return pl.pallas_call(
        _hgrn_kernel,
        grid=(bsz, heads),
        in_specs=[col(0), col(1), col(2), col(3), col(4),
                  pl.BlockSpec((1, A_DK), lambda b, h: (0, h)),
                  pl.BlockSpec((1, A_DK), lambda b, h: (0, 0))],
        out_specs=pl.BlockSpec((1, t, A_DK), lambda b, h: (b, 0, h)),
        out_shape=jax.ShapeDtypeStruct((bsz, t, heads * A_DK), BF16),
        scratch_shapes=[pltpu.VMEM((t, A_DK), F32)] * 2,
        compiler_params=_params(("arbitrary", "arbitrary")),
        name="hgrn2",
    )(z, z, z, z, z, lb.reshape(1, heads * A_DK), norm_g.reshape(1, A_DK))


def _stacked_attention(q, k, v, bias, extra_ok):
    m_rows = q.shape[0]
    heads = q.shape[1] // ATT_DH
    head_of_lane = lax.broadcasted_iota(jnp.int32, (1, q.shape[1]), 1) // ATT_DH
    zero = jnp.zeros((), q.dtype)
    q_bd = jnp.concatenate([jnp.where(head_of_lane == h, q, zero) for h in range(heads)], axis=0)
    s = lax.dot_general(q_bd, k, (((1,), (1,)), ((), ())), preferred_element_type=F32) + bias
    if extra_ok is not None:
        s = jnp.where(extra_ok, s, MASK_VALUE)
    m = jnp.max(s, axis=-1, keepdims=True)
    p = jnp.exp(s - m)
    den = jnp.sum(p, axis=-1, keepdims=True)
    o = jnp.dot(p.astype(BF16), v, preferred_element_type=F32) / den
    lse = m + jnp.log(den)
    out = o[0:m_rows]
    lse_b = jnp.broadcast_to(lse[0:m_rows], out.shape)
    for h in range(1, heads):
        sel = head_of_lane == h
        out = jnp.where(sel, o[h * m_rows:(h + 1) * m_rows], out)
        lse_b = jnp.where(sel, lse[h * m_rows:(h + 1) * m_rows], lse_b)
    return out, lse_b


def _na_kernel(q_ref, k_ref, v_ref, *rest, rows, kr):
    bias_refs, o_ref = rest[:-1], rest[-1]
    nk = kr * GRID_W
    nb = ATT_HPS * GRID_W
    scale = ATT_DH ** -0.5
    for j, bias_ref in enumerate(bias_refs):
        r = pl.program_id(1) * len(bias_refs) + j
        rs = jnp.clip(r - kr // 2, 0, rows - kr)
        k0 = pl.multiple_of(rs * GRID_W, GRID_W)
        q = q_ref[0, j * GRID_W:(j + 1) * GRID_W, :] * scale
        for s in range(q.shape[-1] // ATT_SLAB):
            sl = slice(s * ATT_SLAB, (s + 1) * ATT_SLAB)
            out, _ = _stacked_attention(q[:, sl], k_ref[0, pl.ds(k0, nk), sl], v_ref[0, pl.ds(k0, nk), sl],
                                        bias_ref[0, s * nb:(s + 1) * nb, :], None)
            o_ref[0, j * GRID_W:(j + 1) * GRID_W, sl] = out.astype(o_ref.dtype)


def _na_bias_table(rpb_all, rows):
    layers, heads = rpb_all.shape[:2]
    kr = min(NA_KR, rows)
    col = np.arange(GRID_W)
    cs = np.clip(col - NA_KC // 2, 0, GRID_W - NA_KC)
    kc = np.arange(GRID_W)
    rel = kc[None, :] - col[:, None] + NA_KC - 1
    ok = (kc[None, :] >= cs[:, None]) & (kc[None, :] < cs[:, None] + NA_KC)
    pick_col = ((rel[None] == np.arange(2 * NA_KC - 1)[:, None, None]) & ok[None]).astype(np.float32)
    di = np.arange(kr)[None, :] - np.arange(kr)[:, None] + NA_KR - 1
    pick_row = (di[:, :, None] == np.arange(2 * NA_KR - 1)[None, None, :]).astype(np.float32)
    table = jnp.einsum('lhdr,vid,rck->lvhcik', rpb_all.astype(F32), pick_row, pick_col,
                       precision=lax.Precision.HIGHEST)
    table = jnp.where(ok[None, None, None, :, None, :], table, MASK_VALUE)
    return table.reshape(layers, kr, heads * GRID_W, kr * GRID_W)


def _neighbourhood_attention(zb, table, layer):
    bsz, t, w3 = zb.shape
    width = w3 // 3
    heads = width // ATT_DH
    rows = t // GRID_W
    kr = min(NA_KR, rows)
    rps = NA_ROWS_PER_STEP

    def bias_spec(j):
        def index(b, i):
            r = i * rps + j
            return (layer, r - jnp.clip(r - kr // 2, 0, rows - kr), 0, 0)
        return pl.BlockSpec((None, 1, heads * GRID_W, kr * GRID_W), index)

    return pl.pallas_call(
        functools.partial(_na_kernel, rows=rows, kr=kr),
        grid=(bsz, rows // rps),
        in_specs=[
            pl.BlockSpec((1, rps * GRID_W, width), lambda b, i: (b, i, 0)),
            pl.BlockSpec((1, t, width), lambda b, i: (b, 0, 1)),
            pl.BlockSpec((1, t, width), lambda b, i: (b, 0, 2)),
        ] + [bias_spec(j) for j in range(rps)],
        out_specs=pl.BlockSpec((1, rps * GRID_W, width), lambda b, i: (b, i, 0)),
        out_shape=jax.ShapeDtypeStruct((bsz, t, width), BF16),
        compiler_params=_params(("arbitrary", "arbitrary")),
        name="neighbourhood_attention",
    )(zb, zb, zb, *([table] * rps))


def _dil_kernel(q_ref, kp_ref, k_ref, kn_ref, vp_ref, v_ref, vn_ref, bias_ref, *rest, has_prev, emit_lse):
    n_in = 2 if has_prev else 0
    n_out = 2 if emit_lse else 1
    prev_refs, out_refs = rest[:n_in], rest[n_in:n_in + n_out]
    k_scr, v_scr = rest[n_in + n_out:]
    i = pl.program_id(1)
    last_tile = pl.num_programs(1) - 1
    dil, cs = q_ref.shape[2], q_ref.shape[3]
    width = q_ref.shape[4]
    scale = ATT_DH ** -0.5
    nk = ATT_SUB + 2 * ATT_HALO
    nb = ATT_HPS * ATT_SUB
    n_sub = cs // ATT_SUB
    for r in range(dil):
        k_scr[0:ATT_HALO] = kp_ref[0, 0, r]
        k_scr[ATT_HALO:ATT_HALO + cs] = k_ref[0, 0, r]
        k_scr[ATT_HALO + cs:] = kn_ref[0, 0, r]
        v_scr[0:ATT_HALO] = vp_ref[0, 0, r]
        v_scr[ATT_HALO:ATT_HALO + cs] = v_ref[0, 0, r]
        v_scr[ATT_HALO + cs:] = vn_ref[0, 0, r]
        for sb in range(n_sub):
            r0 = sb * ATT_SUB
            edge = jnp.int32(0)
            if sb == 0:
                edge = edge + jnp.where(i == 0, 1, 0)
            if sb == n_sub - 1:
                edge = edge + jnp.where(i == last_tile, 2, 0)
            q = q_ref[0, 0, r, r0:r0 + ATT_SUB, :] * scale
            if dil == 1:
                rows_out = pl.ds(r0, ATT_SUB)
            else:
                rows_out = pl.ds(r0 * dil + r, ATT_SUB, stride=dil)
            for s in range(width // ATT_SLAB):
                sl = slice(s * ATT_SLAB, (s + 1) * ATT_SLAB)
                out, lse = _stacked_attention(q[:, sl], k_scr[r0:r0 + nk, sl], v_scr[r0:r0 + nk, sl],
                                              bias_ref[edge, s * nb:(s + 1) * nb, :], None)
                for j in range(ATT_SLAB // LANES):
                    ch = s * (ATT_SLAB // LANES) + j
                    o_j, l_j = out[:, j * LANES:(j + 1) * LANES], lse[:, j * LANES:(j + 1) * LANES]
                    if has_prev:
                        o_p, l_p = prev_refs[0][0, ch, rows_out, :], prev_refs[1][0, ch, rows_out, :]
                        m = jnp.maximum(l_p, l_j)
                        e_p, e_j = jnp.exp(l_p - m), jnp.exp(l_j - m)
                        den = e_p + e_j
                        o_j = (e_p * o_p + e_j * o_j) / den
                        l_j = m + jnp.log(den)
                    out_refs[0][0, ch, rows_out, :] = o_j
                    if emit_lse:
                        out_refs[1][0, ch, rows_out, :] = l_j


def _alibi_band(slopes_g, dil):
    nk = ATT_SUB + 2 * ATT_HALO
    rel = np.arange(nk)[None, :] - ATT_HALO - np.arange(ATT_SUB)[:, None]
    band = np.abs(rel) <= ATT_HALO
    dist = jnp.asarray(np.abs(rel) * dil, F32)
    bias = jnp.where(band[None], -slopes_g[:, None, None] * dist[None], MASK_VALUE)
    bias = bias.reshape(slopes_g.shape[0] * ATT_SUB, nk)
    kcol = np.arange(nk)
    before, after = kcol < ATT_HALO, kcol >= ATT_HALO + ATT_SUB
    cut = lambda cols: jnp.where(cols[None, :], MASK_VALUE, bias)
    return jnp.stack([bias, cut(before), cut(after), cut(before | after)])


def _dilated_group(zp, dil, slopes_g, tm, prev, emit_lse):
    bsz, t, f = zp.shape
    width = f // 3
    cs = tm // dil
    nt = t // tm
    zv = zp.reshape(bsz, nt, dil, cs, f)
    bias = _alibi_band(slopes_g, dil)
    tail = cs // ATT_HALO - 1
    prev = () if prev is None else tuple(prev)

    def main(off):
        return pl.BlockSpec((1, 1, dil, cs, width), lambda b, i: (b, i, 0, 0, off))

    def before(off):
        return pl.BlockSpec((1, 1, dil, ATT_HALO, width), lambda b, i: (b, jnp.maximum(i - 1, 0), 0, tail, off))

    def after(off):
        return pl.BlockSpec((1, 1, dil, ATT_HALO, width), lambda b, i: (b, jnp.minimum(i + 1, nt - 1), 0, 0, off))

    tok_spec = pl.BlockSpec((1, width // LANES, tm, LANES), lambda b, i: (b, 0, i, 0))
    n_out = 2 if emit_lse else 1
    return pl.pallas_call(
        functools.partial(_dil_kernel, has_prev=bool(prev), emit_lse=emit_lse),
        grid=(bsz, nt),
        in_specs=[main(0), before(1), main(1), after(1), before(2), main(2), after(2),
                  pl.BlockSpec(bias.shape, lambda b, i: (0, 0, 0))] + [tok_spec] * len(prev),
        out_specs=[tok_spec] * n_out,
        out_shape=[jax.ShapeDtypeStruct((bsz, width // LANES, t, LANES), F32)] * n_out,
        scratch_shapes=[pltpu.VMEM((cs + 2 * ATT_HALO, width), BF16)] * 2,
        compiler_params=_params(("arbitrary", "arbitrary")),
        name=f"dilated_attention_d{dil}",
    )(zv, zv, zv, zv, zv, zv, zv, bias, *prev)


def _pool_kernel(up_ref, u_ref, un_ref, w_ref, sc_ref, o_ref, scr, *, seq_len):
    i = pl.program_id(1)
    tm = u_ref.shape[1]
    gc = w_ref.shape[1]
    scr[0:HALO_ROWS] = jnp.where(i > 0, up_ref[0], 0.0)
    scr[HALO_ROWS:HALO_ROWS + tm] = u_ref[0]
    scr[HALO_ROWS + tm:] = jnp.where(i < pl.num_programs(1) - 1, un_ref[0], 0.0)
    tpos = i * tm + lax.broadcasted_iota(jnp.int32, (tm, 1), 0)
    for g, w in enumerate(POOL_WINDOWS):
        sl = slice(g * gc, (g + 1) * gc)
        run = scr[:, sl]
        span = 1
        while span < w:
            run = run + pltpu.roll(run, span, 0)
            span *= 2
        ahead = w // 2 - 1
        if ahead:
            run = pltpu.roll(run, run.shape[0] - ahead, 0)
        tot = run[HALO_ROWS:HALO_ROWS + tm]
        lo = jnp.clip(tpos - w // 2, 0, seq_len)
        hi = jnp.clip(tpos + w - w // 2, 0, seq_len)
        mean = tot / (hi - lo).astype(F32)
        diff = mean - scr[HALO_ROWS:HALO_ROWS + tm, sl]
        y = jnp.dot(diff.astype(BF16), w_ref[g], preferred_element_type=F32)
        o_ref[0, :, sl] = (y * sc_ref[:, sl]).astype(o_ref.dtype)


def _multiscale_pool(u, w_groups, scale, tm=512):
    bsz, t, ch = u.shape
    hb = tm // HALO_ROWS
    last = t // HALO_ROWS - 1
    return pl.pallas_call(
        functools.partial(_pool_kernel, seq_len=t),
        grid=(bsz, t // tm),
        in_specs=[
            pl.BlockSpec((1, HALO_ROWS, ch), lambda b, i: (b, jnp.maximum(i * hb - 1, 0), 0)),
            pl.BlockSpec((1, tm, ch), lambda b, i: (b, i, 0)),
            pl.BlockSpec((1, HALO_ROWS, ch), lambda b, i: (b, jnp.minimum((i + 1) * hb, last), 0)),
            pl.BlockSpec(w_groups.shape, lambda b, i: (0, 0, 0)),
            pl.BlockSpec((1, ch), lambda b, i: (0, 0)),
        ],
        out_specs=pl.BlockSpec((1, tm, ch), lambda b, i: (b, i, 0)),
        out_shape=jax.ShapeDtypeStruct((bsz, t, ch), BF16),
        scratch_shapes=[pltpu.VMEM((tm + 2 * HALO_ROWS, ch), F32)],
        compiler_params=_params(("arbitrary", "arbitrary")),
        name="multiscale_pool",
    )(u, u, u, w_groups.astype(BF16), scale.reshape(1, ch))


def _out_even_kernel(oa_ref, ob_ref, wa_ref, wb_ref, x_ref, mod_ref, o_ref, *, gate_row):
    mixed = jnp.dot(oa_ref[0], wa_ref[...], preferred_element_type=F32)
    mixed = mixed + jnp.dot(ob_ref[0], wb_ref[...], preferred_element_type=F32)
    gate = mod_ref[0][gate_row:gate_row + 1]
    o_ref[0] = x_ref[0] + gate * mixed


def _out_odd_kernel(oc_ref, od_ref, wa_ref, wb_ref, x_ref, mod_ref, o_ref, *, gate_row):
    oc = jnp.concatenate([oc_ref[0, j] for j in range(oc_ref.shape[1])], axis=1)
    mixed = jnp.dot(oc.astype(BF16), wa_ref[...], preferred_element_type=F32)
    mixed = mixed + jnp.dot(od_ref[0], wb_ref[...], preferred_element_type=F32)
    gate = mod_ref[0][gate_row:gate_row + 1]
    o_ref[0] = x_ref[0] + gate * mixed


def _out_proj(parts, w_all, w_layer, x, mod_all, layer, *, gate_row, odd, tm=1024):
    bsz, t, d = x.shape
    half = w_all.shape[1] // 2
    tok = lambda width: pl.BlockSpec((1, tm, width), lambda b, i: (b, i, 0))
    chunked = pl.BlockSpec((1, half // LANES, tm, LANES), lambda b, i: (b, 0, i, 0))
    kern = _out_odd_kernel if odd else _out_even_kernel
    part_specs = [chunked if p.ndim == 4 else tok(half) for p in parts]
    return pl.pallas_call(
        functools.partial(kern, gate_row=gate_row),
        grid=(bsz, t // tm),
        in_specs=part_specs + [
            pl.BlockSpec((None, half, d), lambda b, i: (w_layer, 0, 0)),
            pl.BlockSpec((None, half, d), lambda b, i: (w_layer, 1, 0)),
            tok(d),
            pl.BlockSpec((None, 1, 6, d), lambda b, i: (layer, b, 0, 0)),
        ],
        out_specs=tok(d),
        out_shape=jax.ShapeDtypeStruct((bsz, t, d), F32),
        compiler_params=_params(("arbitrary", "arbitrary")),
        name="out_proj_odd" if odd else "out_proj_even",
    )(*parts, w_all, w_all, x, mod_all)


def _ffn_up_kernel(xp_ref, x_ref, xn_ref, g_ref, mod_ref, w_ref, cw_ref, cb_ref, o_ref, h_scr, *, shift_row):
    i = pl.program_id(1)
    tm = x_ref.shape[1]
    dff = o_ref.shape[2]
    rows = tm + 2 * HALO_ROWS
    m = mod_ref[0]
    shift, scale = m[shift_row:shift_row + 1], m[shift_row + 1:shift_row + 2]
    g = g_ref[...]
    h_scr[0:HALO_ROWS] = _norm_mod(xp_ref[0], g, shift, scale).astype(BF16)
    h_scr[HALO_ROWS:HALO_ROWS + tm] = _norm_mod(x_ref[0], g, shift, scale).astype(BF16)
    h_scr[HALO_ROWS + tm:] = _norm_mod(xn_ref[0], g, shift, scale).astype(BF16)
    top = jnp.where(i > 0, 1.0, 0.0)
    bottom = jnp.where(i < pl.num_programs(1) - 1, 1.0, 0.0)
    h = h_scr[...]
    for c0 in range(0, dff, FFN_COLS):
        gate = jnp.dot(h, w_ref[:, c0:c0 + FFN_COLS], preferred_element_type=F32)
        gate = jnp.concatenate([gate[:HALO_ROWS] * top, gate[HALO_ROWS:HALO_ROWS + tm],
                                gate[HALO_ROWS + tm:] * bottom], axis=0)
        val = jnp.dot(h, w_ref[:, dff + c0:dff + c0 + FFN_COLS], preferred_element_type=F32)
        cw = 0.5 * cw_ref[:, c0:c0 + FFN_COLS]
        half = 0.5 * cb_ref[:, c0:c0 + FFN_COLS] + gate * cw[1:2]
        half = half + pltpu.roll(gate, 1, 0) * cw[0:1]
        half = half + pltpu.roll(gate, rows - 1, 0) * cw[2:3]
        act = half * (1.0 + jnp.tanh(half)) * val
        o_ref[0, :, c0:c0 + FFN_COLS] = act[HALO_ROWS:HALO_ROWS + tm].astype(o_ref.dtype)


def _ffn_up(x, g_all, mod_all, w_all, conv_w_all, conv_b_all, layer, *, shift_row, tm=1024):
    bsz, t, d = x.shape
    dff = w_all.shape[2] // 2
    hb = tm // HALO_ROWS
    last = t // HALO_ROWS - 1
    return pl.pallas_call(
        functools.partial(_ffn_up_kernel, shift_row=shift_row),
        grid=(bsz, t // tm),
        in_specs=[
            pl.BlockSpec((1, HALO_ROWS, d), lambda b, i: (b, jnp.maximum(i * hb - 1, 0), 0)),
            pl.BlockSpec((1, tm, d), lambda b, i: (b, i, 0)),
            pl.BlockSpec((1, HALO_ROWS, d), lambda b, i: (b, jnp.minimum((i + 1) * hb, last), 0)),
            pl.BlockSpec((None, 1, d), lambda b, i: (layer, 0, 0)),
            pl.BlockSpec((None, 1, 6, d), lambda b, i: (layer, b, 0, 0)),
            pl.BlockSpec((None, d, 2 * dff), lambda b, i: (layer, 0, 0), pipeline_mode=pl.Buffered(1)),
            pl.BlockSpec((None, FFN_CONV, dff), lambda b, i: (layer, 0, 0)),
            pl.BlockSpec((None, 1, dff), lambda b, i: (layer, 0, 0)),
        ],
        out_specs=pl.BlockSpec((1, tm, dff), lambda b, i: (b, i, 0)),
        out_shape=jax.ShapeDtypeStruct((bsz, t, dff), BF16),
        scratch_shapes=[pltpu.VMEM((tm + 2 * HALO_ROWS, d), BF16)],
        compiler_params=_params(("arbitrary", "arbitrary")),
        name="ffn_up_conv",
    )(x, x, x, g_all, mod_all, w_all, conv_w_all, conv_b_all)


def _ffn_down_kernel(a_ref, w_ref, x_ref, mod_ref, *rest, gate_row):
    y = jnp.dot(a_ref[0], w_ref[...], preferred_element_type=F32)
    x = x_ref[0] + mod_ref[0][gate_row:gate_row + 1] * y
    if len(rest) == 2:
        g_ref, o_ref = rest
        x = x * lax.rsqrt(jnp.mean(x * x, axis=-1, keepdims=True) + RMS_EPS) * g_ref[...]
    else:
        o_ref, = rest
    o_ref[0] = x


def _ffn_down(a, w_all, x, mod_all, layer, final_g, *, gate_row, tm=512):
    bsz, t, d = x.shape
    dff = a.shape[-1]
    extra = [] if final_g is None else [final_g]
    return pl.pallas_call(
        functools.partial(_ffn_down_kernel, gate_row=gate_row),
        grid=(bsz, t // tm),
        in_specs=[
            pl.BlockSpec((1, tm, dff), lambda b, i: (b, i, 0)),
            pl.BlockSpec((None, dff, d), lambda b, i: (layer, 0, 0)),
            pl.BlockSpec((1, tm, d), lambda b, i: (b, i, 0)),
            pl.BlockSpec((None, 1, 6, d), lambda b, i: (layer, b, 0, 0)),
        ] + [pl.BlockSpec((1, d), lambda b, i: (0, 0)) for _ in extra],
        out_specs=pl.BlockSpec((1, tm, d), lambda b, i: (b, i, 0)),
        out_shape=jax.ShapeDtypeStruct((bsz, t, d), F32),
        compiler_params=_params(("arbitrary", "arbitrary")),
        name="ffn_down",
    )(a, w_all, x, mod_all, *extra)


def kernel(x, c, ada_w, ada_b, norm_mix_g, norm_ffn_g, even_w_in, even_w_out, hgrn_lb_logits, hgrn_norm_g, na_rpb, odd_w_in, odd_w_out, pool_w, pool_scale, ffn_w_up, ffn_conv_w, ffn_conv_b, ffn_w_down, final_norm_g):
    bsz, t, d = x.shape
    depth = ada_w.shape[0]
    a_width = hgrn_lb_logits.shape[1]
    n_groups = len(C_CONFIGS)
    c_width = C_HPG * ATT_DH
    n_c = 3 * n_groups * c_width

    lb_soft = jax.nn.softmax(hgrn_lb_logits.astype(F32), axis=0)
    lower_bounds = jnp.cumsum(lb_soft, axis=0) - lb_soft[0]
    slopes = jnp.exp2(-8.0 * jnp.arange(1, n_groups * C_HPG + 1, dtype=F32) / (n_groups * C_HPG))
    slopes = slopes.reshape(n_groups, C_HPG)

    mod_all = _ada_mod(c, ada_w, ada_b).reshape(depth, bsz, 6, d)

    g_mix, g_ffn = norm_mix_g.reshape(depth, 1, d), norm_ffn_g.reshape(depth, 1, d)
    even_in, odd_in = even_w_in.astype(BF16), odd_w_in.astype(BF16)
    even_out, odd_out = even_w_out.astype(BF16), odd_w_out.astype(BF16)
    w_up, w_down = ffn_w_up.astype(BF16), ffn_w_down.astype(BF16)
    conv_b = ffn_conv_b.reshape(depth, 1, -1)
    na_tables = _na_bias_table(na_rpb, t // GRID_W)
    qkv = lambda g: tuple(((k * n_groups + g) * c_width, c_width) for k in range(3))

    for l in range(depth):
        if l % 2 == 0:
            e = l // 2
            b_width = (even_in.shape[2] - 5 * a_width) // 3
            b_cols = tuple((5 * a_width + k * b_width, b_width) for k in range(3))
            z_a, z_b = _in_proj(x, g_mix, mod_all, even_in, l, e, (((0, 5 * a_width),), b_cols),
                                (F32, BF16), tm=512)
            o_a = _hgrn2(z_a, lower_bounds[e], hgrn_norm_g[e])
            o_b = _neighbourhood_attention(z_b, na_tables, e)
            x = _out_proj([o_a, o_b], even_out, e, x, mod_all, l, gate_row=2, odd=False)
        else:
            o_i = l // 2
            acc = None
            for g, (_, dil) in enumerate(C_CONFIGS):
                if dil == 1:
                    z_g, z_d = _in_proj(x, g_mix, mod_all, odd_in, l, o_i, (qkv(g), ((n_c, odd_in.shape[2] - n_c),)),
                                        (BF16, F32), tm=DIL_TILE[dil])
                else:
                    z_g, = _in_proj(x, g_mix, mod_all, odd_in, l, o_i, (qkv(g),), (BF16,),
                                    tm=DIL_TILE[dil], perm=dil)
                acc = _dilated_group(z_g, dil, slopes[g], DIL_TILE[dil], acc, emit_lse=g < n_groups - 1)
            o_d = _multiscale_pool(z_d, pool_w[o_i], pool_scale[o_i])
            x = _out_proj([acc[0], o_d], odd_out, o_i, x, mod_all, l, gate_row=2, odd=True)
        a = _ffn_up(x, g_ffn, mod_all, w_up, ffn_conv_w, conv_b, l, shift_row=3)
        final_g = final_norm_g.reshape(1, d) if l == depth - 1 else None
        x = _ffn_down(a, w_down, x, mod_all, l, final_g, gate_row=5)
    return x
```

```python
import functools

import jax
import jax.numpy as jnp
import numpy as np
from jax import lax
from jax.experimental import pallas as pl
from jax.experimental.pallas import tpu as pltpu

F32 = jnp.float32
BF16 = jnp.bfloat16

GRID_W = 64
RMS_EPS = 1e-6
LB_FLOOR = 1e-12
MASK_VALUE = -1e30
A_DK = 128
HGRN_TILE = 128
HGRN_GROUP = 4
HGRN_LEVELS = (4, 8, 16, 32, 64)
ATT_DH = 64
ATT_SLAB = 256
ATT_HPS = ATT_SLAB // ATT_DH
NA_KR = 8
NA_KC = 16
NA_ROWS_PER_STEP = 8
C_CONFIGS = ((128, 1), (512, 4), (2048, 16))
C_HPG = 8
DIL_TILE = {1: 1024, 4: 1024, 16: 2048}
ATT_SUB = 128
ATT_HALO = 64
POOL_WINDOWS = (2, 4, 8, 16)
POOL_HALO = 8
FFN_CONV = 3
FFN_COLS = 256
HALO_ROWS = 8
LANES = 128
assert all(w & (w - 1) == 0 and w // 2 <= HALO_ROWS for w in POOL_WINDOWS)
VMEM_LIMIT = 56 * 1024 * 1024


def _params(sem):
    return pltpu.CompilerParams(dimension_semantics=sem, vmem_limit_bytes=VMEM_LIMIT)


def _silu(v):
    return v * (1.0 / (1.0 + jnp.exp(-v)))


def _norm_mod(x, g, shift, scale):
    ms = jnp.mean(x * x, axis=-1, keepdims=True)
    return x * lax.rsqrt(ms + RMS_EPS) * (g * (1.0 + scale)) + shift


def _ada_kernel(c_ref, w_ref, b_ref, o_ref):
    c = c_ref[...]
    ca = _silu(c).astype(BF16)
    o_ref[0] = jnp.dot(ca, w_ref[0].astype(BF16), preferred_element_type=F32) + b_ref[0]


def _ada_mod(c, ada_w, ada_b):
    depth, d, n = ada_w.shape
    bsz = c.shape[0]
    tn = 1024
    return pl.pallas_call(
        _ada_kernel,
        grid=(depth, n // tn),
        in_specs=[
            pl.BlockSpec((bsz, d), lambda l, j: (0, 0)),
            pl.BlockSpec((1, d, tn), lambda l, j: (l, 0, j)),
            pl.BlockSpec((1, 1, tn), lambda l, j: (l, 0, j)),
        ],
        out_specs=pl.BlockSpec((1, bsz, tn), lambda l, j: (l, 0, j)),
        out_shape=jax.ShapeDtypeStruct((depth, bsz, n), F32),
        compiler_params=_params(("arbitrary", "arbitrary")),
        name="ada_mod",
    )(c, ada_w, ada_b.reshape(depth, 1, n))


def _in_proj_kernel(x_ref, g_ref, mod_ref, *rest, shift_row, perm, col_groups):
    n_out = len(col_groups)
    n_w = sum(len(ranges) for ranges in col_groups)
    w_refs, rest = rest[:n_w], rest[n_w:]
    o_refs, h_ref = rest[:n_out], rest[n_out]
    m = mod_ref[0]
    h = _norm_mod(x_ref[0], g_ref[...], m[shift_row:shift_row + 1], m[shift_row + 1:shift_row + 2])
    tm = h.shape[0]
    if perm == 1:
        h_ref[...] = h.astype(BF16)
        parts = 1
    else:
        h32 = rest[n_out + 1]
        for j in range(h32.shape[0]):
            h32[j] = h[:, j * LANES:(j + 1) * LANES]
        parts = 2
    cs = tm // perm
    rows = tm // parts
    for p in range(parts):
        if perm > 1:
            for r in range(p * perm // parts, (p + 1) * perm // parts):
                for j in range(h32.shape[0]):
                    h_ref[r * cs:(r + 1) * cs, j * LANES:(j + 1) * LANES] = (
                        h32[j, pl.ds(r, cs, stride=perm), :].astype(BF16))
        hb = h_ref[p * rows:(p + 1) * rows]
        w_iter = iter(w_refs)
        for o_ref, ranges in zip(o_refs, col_groups):
            off = 0
            for _, width in ranges:
                o_ref[0, p * rows:(p + 1) * rows, off:off + width] = jnp.dot(
                    hb, next(w_iter)[...], preferred_element_type=F32).astype(o_ref.dtype)
                off += width


def _in_proj(x, g_all, mod_all, w_all, layer, w_layer, col_groups, out_dtypes, *, tm=1024, perm=1):
    bsz, t, d = x.shape
    widths = [sum(w for _, w in ranges) for ranges in col_groups]
    w_specs = []
    for ranges in col_groups:
        for start, width in ranges:
            assert start % width == 0
            w_specs.append(pl.BlockSpec((None, d, width), lambda b, i, blk=start // width: (w_layer, 0, blk),
                                        pipeline_mode=pl.Buffered(1)))
    return pl.pallas_call(
        functools.partial(_in_proj_kernel, shift_row=0, perm=perm, col_groups=col_groups),
        grid=(bsz, t // tm),
        in_specs=[
            pl.BlockSpec((1, tm, d), lambda b, i: (b, i, 0)),
            pl.BlockSpec((None, 1, d), lambda b, i: (layer, 0, 0)),
            pl.BlockSpec((None, 1, 6, d), lambda b, i: (layer, b, 0, 0)),
        ] + w_specs,
        out_specs=[pl.BlockSpec((1, tm, w), lambda b, i: (b, i, 0)) for w in widths],
        out_shape=[jax.ShapeDtypeStruct((bsz, t, w), dt) for w, dt in zip(widths, out_dtypes)],
        scratch_shapes=[pltpu.VMEM((tm, d), BF16)] + ([pltpu.VMEM((d // LANES, tm, LANES), F32)] if perm > 1 else []),
        compiler_params=_params(("arbitrary", "arbitrary")),
        name="in_proj",
    )(x, g_all, mod_all, *([w_all] * len(w_specs)))


def _split3(x):
    hi = x.astype(BF16)
    r = x - hi.astype(F32)
    mid = r.astype(BF16)
    lo = (r - mid.astype(F32)).astype(BF16)
    return hi, mid, lo


def _span_code(later, earlier):
    g = HGRN_GROUP
    code = jnp.where(((later // g) == (earlier // g)) & (later >= earlier), later - earlier, -1)
    for li, m in enumerate(HGRN_LEVELS):
        straddles = ((later // (2 * m)) == (earlier // (2 * m))) & ((later % (2 * m)) >= m) & ((earlier % (2 * m)) < m)
        code = jnp.where(straddles, g + li, code)
    return code


def _hgrn_kernel(q_ref, ff_ref, fb_ref, v_ref, g_ref, lb_ref, ng_ref, o_ref, acc_scr, qs_scr):
    t = q_ref.shape[1]
    tl = HGRN_TILE
    g8 = HGRN_GROUP
    n_tiles = t // tl
    log2e = 1.4426950408889634

    lb = lb_ref[...]
    lb_floor = jnp.maximum(lb, LB_FLOOR)
    one_m_lb = 1.0 - lb
    log2_one_m_lb = jnp.log2(one_m_lb)

    row = lax.broadcasted_iota(jnp.int32, (tl, tl), 0)
    col = lax.broadcasted_iota(jnp.int32, (tl, tl), 1)
    tri = jnp.where(col <= row, 1.0, 0.0).astype(BF16)
    code_f = _span_code(row, col)
    code_b = _span_code(col, row)
    r2 = lax.broadcasted_iota(jnp.int32, (2 * tl, 2 * tl), 0) < tl
    c2 = lax.broadcasted_iota(jnp.int32, (2 * tl, 2 * tl), 1) < tl
    ones_bd = jnp.where(r2 == c2, 1.0, 0.0).astype(BF16)

    def gates(fz):
        e = jnp.exp(-jnp.abs(fz))
        r = 1.0 / (1.0 + e)
        pos = fz >= 0
        sig_pos = jnp.where(pos, r, e * r)
        sig_neg = jnp.where(pos, e * r, r)
        log2_sig_neg = -(jnp.maximum(fz, 0.0) * log2e + jnp.log2(1.0 + e))
        return jnp.log2(lb_floor + one_m_lb * sig_pos), one_m_lb * sig_neg, log2_one_m_lb + log2_sig_neg

    def rot_group(x, shift):
        return pltpu.roll(x.reshape(tl // HALO_ROWS, HALO_ROWS, x.shape[1]), shift % HALO_ROWS, 1).reshape(x.shape)

    def span_ref(b, fwd, m):
        blocks = []
        for a in range(0, tl, 2 * m):
            edge = a + m - 1 if fwd else a + m
            blocks.append(jnp.broadcast_to(b[edge:edge + 1], (2 * m, b.shape[1])))
        return jnp.concatenate(blocks, axis=0)

    acc_scr[...] = jnp.zeros_like(acc_scr)

    def silu_body(i, carry):
        t0 = pl.multiple_of(i * tl, tl)
        qs_scr[pl.ds(t0, tl), :] = _silu(q_ref[0, pl.ds(t0, tl), :])
        return carry

    lax.fori_loop(0, n_tiles, silu_body, 0, unroll=4)

    def body(n, carry):
        st_f, st_b = carry
        tf = pl.multiple_of(n * tl, tl)
        tb = pl.multiple_of((n_tiles - 1 - n) * tl, tl)
        q_f = qs_scr[pl.ds(tf, tl), :]
        q_b = qs_scr[pl.ds(tb, tl), :]
        v_f = v_ref[0, pl.ds(tf, tl), :].astype(BF16)
        v_b = v_ref[0, pl.ds(tb, tl), :].astype(BF16)
        lf_f, kk_f, lk_f = gates(ff_ref[0, pl.ds(tf, tl), :])
        lf_b, kk_b, lk_b = gates(fb_ref[0, pl.ds(tb, tl), :])

        pieces = jnp.concatenate(_split3(lf_f) + _split3(lf_b), axis=1)
        ps = jnp.dot(tri, pieces, preferred_element_type=F32)
        b_f = ps[:, 0:A_DK] + ps[:, A_DK:2 * A_DK] + ps[:, 2 * A_DK:3 * A_DK]
        p_b = ps[:, 3 * A_DK:4 * A_DK] + ps[:, 4 * A_DK:5 * A_DK] + ps[:, 5 * A_DK:6 * A_DK]
        b_b = p_b[tl - 1:tl] - p_b + lf_b

        w_f, w_b = lk_f - b_f, lk_b - b_b
        lhs = []
        for e in range(g8):
            wf_e = w_f if e == 0 else rot_group(w_f, e)
            wb_e = w_b if e == 0 else rot_group(w_b, HALO_ROWS - e)
            pf = q_f * jnp.exp2(jnp.minimum(b_f + wf_e, 0.0))
            pb = q_b * jnp.exp2(jnp.minimum(b_b + wb_e, 0.0))
            lhs.append(jnp.concatenate([pf, pb], axis=1).astype(BF16))
        half = g8 // 2
        att = [jnp.dot(jnp.concatenate(part, axis=0), ones_bd, preferred_element_type=F32)
               for part in (lhs[:half], lhs[half:])]
        a_f = jnp.zeros((tl, tl), F32)
        a_b = jnp.zeros((tl, tl), F32)
        for e in range(g8):
            blk = att[e // half][(e % half) * tl:(e % half + 1) * tl]
            a_f = jnp.where(code_f == e, blk[:, :A_DK], a_f)
            a_b = jnp.where(code_b == e, blk[:, A_DK:], a_b)

        for li, m in enumerate(HGRN_LEVELS):
            e_f = jnp.exp2(-jnp.abs(b_f - span_ref(b_f, True, m)))
            e_b = jnp.exp2(-jnp.abs(b_b - span_ref(b_b, False, m)))
            s_f = lax.dot_general((q_f * e_f).astype(BF16), (kk_f * e_f).astype(BF16),
                                  (((1,), (1,)), ((), ())), preferred_element_type=F32)
            s_b = lax.dot_general((q_b * e_b).astype(BF16), (kk_b * e_b).astype(BF16),
                                  (((1,), (1,)), ((), ())), preferred_element_type=F32)
            a_f = jnp.where(code_f == g8 + li, s_f, a_f)
            a_b = jnp.where(code_b == g8 + li, s_b, a_b)

        def inter(q, v_bf, kk, b, edge, st):
            b_edge = b[edge:edge + 1]
            qb = (q * jnp.exp2(b)).astype(BF16)
            o = lax.dot_general(qb, st.astype(BF16), (((1,), (1,)), ((), ())), preferred_element_type=F32)
            kd = (kk * jnp.exp2(b_edge - b)).astype(BF16)
            kv = lax.dot_general(v_bf, kd, (((0,), (0,)), ((), ())), preferred_element_type=F32)
            return o, st * jnp.exp2(b_edge) + kv

        oi_f, st_f = inter(q_f, v_f, kk_f, b_f, tl - 1, st_f)
        oi_b, st_b = inter(q_b, v_b, kk_b, b_b, 0, st_b)
        acc_scr[pl.ds(tf, tl), :] += jnp.dot(a_f.astype(BF16), v_f, preferred_element_type=F32) + oi_f
        acc_scr[pl.ds(tb, tl), :] += jnp.dot(a_b.astype(BF16), v_b, preferred_element_type=F32) + oi_b
        return st_f, st_b

    zero = jnp.zeros((A_DK, A_DK), F32)
    lax.fori_loop(0, n_tiles, body, (zero, zero), unroll=8)

    ng = ng_ref[...]

    def out_body(i, carry):
        t0 = pl.multiple_of(i * tl, tl)
        o = acc_scr[pl.ds(t0, tl), :]
        o = o * lax.rsqrt(jnp.mean(o * o, axis=-1, keepdims=True) + RMS_EPS) * ng
        o_ref[0, pl.ds(t0, tl), :] = (o * _silu(g_ref[0, pl.ds(t0, tl), :])).astype(o_ref.dtype)
        return carry

    lax.fori_loop(0, n_tiles, out_body, 0, unroll=4)


def _hgrn2(z, lb, norm_g):
    bsz, t, w5 = z.shape
    heads = w5 // 5 // A_DK

    def col(k):
        return pl.BlockSpec((1, t, A_DK), lambda b, h, k=k: (b, 0, h + heads * k))

    return pl.pallas_call(
        _hgrn_kernel,
        grid=(bsz, heads),
        in_specs=[col(0), col(1), col(2), col(3), col(4),
                  pl.BlockSpec((1, A_DK), lambda b, h: (0, h)),
                  pl.BlockSpec((1, A_DK), lambda b, h: (0, 0))],
        out_specs=pl.BlockSpec((1, t, A_DK), lambda b, h: (b, 0, h)),
        out_shape=jax.ShapeDtypeStruct((bsz, t, heads * A_DK), BF16),
        scratch_shapes=[pltpu.VMEM((t, A_DK), F32)] * 2,
        compiler_params=_params(("arbitrary", "arbitrary")),
        name="hgrn2",
    )(z, z, z, z, z, lb.reshape(1, heads * A_DK), norm_g.reshape(1, A_DK))


def _stacked_attention(q, k, v, bias, extra_ok):
    m_rows = q.shape[0]
    heads = q.shape[1] // ATT_DH
    head_of_lane = lax.broadcasted_iota(jnp.int32, (1, q.shape[1]), 1) // ATT_DH
    zero = jnp.zeros((), q.dtype)
    q_bd = jnp.concatenate([jnp.where(head_of_lane == h, q, zero) for h in range(heads)], axis=0)
    s = lax.dot_general(q_bd, k, (((1,), (1,)), ((), ())), preferred_element_type=F32) + bias
    if extra_ok is not None:
        s = jnp.where(extra_ok, s, MASK_VALUE)
    m = jnp.max(s, axis=-1, keepdims=True)
    p = jnp.exp(s - m)
    den = jnp.sum(p, axis=-1, keepdims=True)
    o = jnp.dot(p.astype(BF16), v, preferred_element_type=F32) / den
    lse = m + jnp.log(den)
    out = o[0:m_rows]
    lse_b = jnp.broadcast_to(lse[0:m_rows], out.shape)
    for h in range(1, heads):
        sel = head_of_lane == h
        out = jnp.where(sel, o[h * m_rows:(h + 1) * m_rows], out)
        lse_b = jnp.where(sel, lse[h * m_rows:(h + 1) * m_rows], lse_b)
    return out, lse_b


def _na_kernel(q_ref, k_ref, v_ref, *rest, rows, kr):
    bias_refs, o_ref = rest[:-1], rest[-1]
    nk = kr * GRID_W
    nb = ATT_HPS * GRID_W
    scale = ATT_DH ** -0.5
    for j, bias_ref in enumerate(bias_refs):
        r = pl.program_id(1) * len(bias_refs) + j
        rs = jnp.clip(r - kr // 2, 0, rows - kr)
        k0 = pl.multiple_of(rs * GRID_W, GRID_W)
        q = q_ref[0, j * GRID_W:(j + 1) * GRID_W, :] * scale
        for s in range(q.shape[-1] // ATT_SLAB):
            sl = slice(s * ATT_SLAB, (s + 1) * ATT_SLAB)
            out, _ = _stacked_attention(q[:, sl], k_ref[0, pl.ds(k0, nk), sl], v_ref[0, pl.ds(k0, nk), sl],
                                        bias_ref[0, s * nb:(s + 1) * nb, :], None)
            o_ref[0, j * GRID_W:(j + 1) * GRID_W, sl] = out.astype(o_ref.dtype)


def _na_bias_table(rpb_all, rows):
    layers, heads = rpb_all.shape[:2]
    kr = min(NA_KR, rows)
    col = np.arange(GRID_W)
    cs = np.clip(col - NA_KC // 2, 0, GRID_W - NA_KC)
    kc = np.arange(GRID_W)
    rel = kc[None, :] - col[:, None] + NA_KC - 1
    ok = (kc[None, :] >= cs[:, None]) & (kc[None, :] < cs[:, None] + NA_KC)
    pick_col = ((rel[None] == np.arange(2 * NA_KC - 1)[:, None, None]) & ok[None]).astype(np.float32)
    di = np.arange(kr)[None, :] - np.arange(kr)[:, None] + NA_KR - 1
    pick_row = (di[:, :, None] == np.arange(2 * NA_KR - 1)[None, None, :]).astype(np.float32)
    table = jnp.einsum('lhdr,vid,rck->lvhcik', rpb_all.astype(F32), pick_row, pick_col,
                       precision=lax.Precision.HIGHEST)
    table = jnp.where(ok[None, None, None, :, None, :], table, MASK_VALUE)
    return table.reshape(layers, kr, heads * GRID_W, kr * GRID_W)


def _neighbourhood_attention(zb, table, layer):
    bsz, t, w3 = zb.shape
    width = w3 // 3
    heads = width // ATT_DH
    rows = t // GRID_W
    kr = min(NA_KR, rows)
    rps = NA_ROWS_PER_STEP

    def bias_spec(j):
        def index(b, i):
            r = i * rps + j
            return (layer, r - jnp.clip(r - kr // 2, 0, rows - kr), 0, 0)
        return pl.BlockSpec((None, 1, heads * GRID_W, kr * GRID_W), index)

    return pl.pallas_call(
        functools.partial(_na_kernel, rows=rows, kr=kr),
        grid=(bsz, rows // rps),
        in_specs=[
            pl.BlockSpec((1, rps * GRID_W, width), lambda b, i: (b, i, 0)),
            pl.BlockSpec((1, t, width), lambda b, i: (b, 0, 1)),
            pl.BlockSpec((1, t, width), lambda b, i: (b, 0, 2)),
        ] + [bias_spec(j) for j in range(rps)],
        out_specs=pl.BlockSpec((1, rps * GRID_W, width), lambda b, i: (b, i, 0)),
        out_shape=jax.ShapeDtypeStruct((bsz, t, width), BF16),
        compiler_params=_params(("arbitrary", "arbitrary")),
        name="neighbourhood_attention",
    )(zb, zb, zb, *([table] * rps))


def _dil_kernel(q_ref, kp_ref, k_ref, kn_ref, vp_ref, v_ref, vn_ref, bias_ref, *rest, has_prev, emit_lse):
    n_in = 2 if has_prev else 0
    n_out = 2 if emit_lse else 1
    prev_refs, out_refs = rest[:n_in], rest[n_in:n_in + n_out]
    k_scr, v_scr = rest[n_in + n_out:]
    i = pl.program_id(1)
    last_tile = pl.num_programs(1) - 1
    dil, cs = q_ref.shape[2], q_ref.shape[3]
    width = q_ref.shape[4]
    scale = ATT_DH ** -0.5
    nk = ATT_SUB + 2 * ATT_HALO
    nb = ATT_HPS * ATT_SUB
    n_sub = cs // ATT_SUB
    for r in range(dil):
        k_scr[0:ATT_HALO] = kp_ref[0, 0, r]
        k_scr[ATT_HALO:ATT_HALO + cs] = k_ref[0, 0, r]
        k_scr[ATT_HALO + cs:] = kn_ref[0, 0, r]
        v_scr[0:ATT_HALO] = vp_ref[0, 0, r]
        v_scr[ATT_HALO:ATT_HALO + cs] = v_ref[0, 0, r]
        v_scr[ATT_HALO + cs:] = vn_ref[0, 0, r]
        for sb in range(n_sub):
            r0 = sb * ATT_SUB
            edge = jnp.int32(0)
            if sb == 0:
                edge = edge + jnp.where(i == 0, 1, 0)
            if sb == n_sub - 1:
                edge = edge + jnp.where(i == last_tile, 2, 0)
            q = q_ref[0, 0, r, r0:r0 + ATT_SUB, :] * scale
            if dil == 1:
                rows_out = pl.ds(r0, ATT_SUB)
            else:
                rows_out = pl.ds(r0 * dil + r, ATT_SUB, stride=dil)
            for s in range(width // ATT_SLAB):
                sl = slice(s * ATT_SLAB, (s + 1) * ATT_SLAB)
                out, lse = _stacked_attention(q[:, sl], k_scr[r0:r0 + nk, sl], v_scr[r0:r0 + nk, sl],
                                              bias_ref[edge, s * nb:(s + 1) * nb, :], None)
                for j in range(ATT_SLAB // LANES):
                    ch = s * (ATT_SLAB // LANES) + j
                    o_j, l_j = out[:, j * LANES:(j + 1) * LANES], lse[:, j * LANES:(j + 1) * LANES]
                    if has_prev:
                        o_p, l_p = prev_refs[0][0, ch, rows_out, :], prev_refs[1][0, ch, rows_out, :]
                        m = jnp.maximum(l_p, l_j)
                        e_p, e_j = jnp.exp(l_p - m), jnp.exp(l_j - m)
                        den = e_p + e_j
                        o_j = (e_p * o_p + e_j * o_j) / den
                        l_j = m + jnp.log(den)
                    out_refs[0][0, ch, rows_out, :] = o_j
                    if emit_lse:
                        out_refs[1][0, ch, rows_out, :] = l_j


def _alibi_band(slopes_g, dil):
    nk = ATT_SUB + 2 * ATT_HALO
    rel = np.arange(nk)[None, :] - ATT_HALO - np.arange(ATT_SUB)[:, None]
    band = np.abs(rel) <= ATT_HALO
    dist = jnp.asarray(np.abs(rel) * dil, F32)
    bias = jnp.where(band[None], -slopes_g[:, None, None] * dist[None], MASK_VALUE)
    bias = bias.reshape(slopes_g.shape[0] * ATT_SUB, nk)
    kcol = np.arange(nk)
    before, after = kcol < ATT_HALO, kcol >= ATT_HALO + ATT_SUB
    cut = lambda cols: jnp.where(cols[None, :], MASK_VALUE, bias)
    return jnp.stack([bias, cut(before), cut(after), cut(before | after)])


def _dilated_group(zp, dil, slopes_g, tm, prev, emit_lse):
    bsz, t, f = zp.shape
    width = f // 3
    cs = tm // dil
    nt = t // tm
    zv = zp.reshape(bsz, nt, dil, cs, f)
    bias = _alibi_band(slopes_g, dil)
    tail = cs // ATT_HALO - 1
    prev = () if prev is None else tuple(prev)

    def main(off):
        return pl.BlockSpec((1, 1, dil, cs, width), lambda b, i: (b, i, 0, 0, off))

    def before(off):
        return pl.BlockSpec((1, 1, dil, ATT_HALO, width), lambda b, i: (b, jnp.maximum(i - 1, 0), 0, tail, off))

    def after(off):
        return pl.BlockSpec((1, 1, dil, ATT_HALO, width), lambda b, i: (b, jnp.minimum(i + 1, nt - 1), 0, 0, off))

    tok_spec = pl.BlockSpec((1, width // LANES, tm, LANES), lambda b, i: (b, 0, i, 0))
    n_out = 2 if emit_lse else 1
    return pl.pallas_call(
        functools.partial(_dil_kernel, has_prev=bool(prev), emit_lse=emit_lse),
        grid=(bsz, nt),
        in_specs=[main(0), before(1), main(1), after(1), before(2), main(2), after(2),
                  pl.BlockSpec(bias.shape, lambda b, i: (0, 0, 0))] + [tok_spec] * len(prev),
        out_specs=[tok_spec] * n_out,
        out_shape=[jax.ShapeDtypeStruct((bsz, width // LANES, t, LANES), F32)] * n_out,
        scratch_shapes=[pltpu.VMEM((cs + 2 * ATT_HALO, width), BF16)] * 2,
        compiler_params=_params(("arbitrary", "arbitrary")),
        name=f"dilated_attention_d{dil}",
    )(zv, zv, zv, zv, zv, zv, zv, bias, *prev)


def _pool_tile(up_ref, u_ref, un_ref, w_ref, sc_ref, scr, seq_len):
    i = pl.program_id(1)
    tm = u_ref.shape[1]
    gc = w_ref.shape[1]
    scr[0:HALO_ROWS] = jnp.where(i > 0, up_ref[0], 0.0)
    scr[HALO_ROWS:HALO_ROWS + tm] = u_ref[0]
    scr[HALO_ROWS + tm:] = jnp.where(i < pl.num_programs(1) - 1, un_ref[0], 0.0)
    tpos = i * tm + lax.broadcasted_iota(jnp.int32, (tm, 1), 0)
    outs = []
    for g, w in enumerate(POOL_WINDOWS):
        sl = slice(g * gc, (g + 1) * gc)
        run = scr[:, sl]
        span = 1
        while span < w:
            run = run + pltpu.roll(run, span, 0)
            span *= 2
        ahead = w // 2 - 1
        if ahead:
            run = pltpu.roll(run, run.shape[0] - ahead, 0)
        tot = run[HALO_ROWS:HALO_ROWS + tm]
        lo = jnp.clip(tpos - w // 2, 0, seq_len)
        hi = jnp.clip(tpos + w - w // 2, 0, seq_len)
        mean = tot / (hi - lo).astype(F32)
        diff = mean - scr[HALO_ROWS:HALO_ROWS + tm, sl]
        y = jnp.dot(diff.astype(BF16), w_ref[g], preferred_element_type=F32)
        outs.append(y * sc_ref[:, sl])
    return jnp.concatenate(outs, axis=1)


def _out_even_kernel(oa_ref, ob_ref, wa_ref, wb_ref, x_ref, mod_ref, o_ref, *, gate_row):
    mixed = jnp.dot(oa_ref[0], wa_ref[...], preferred_element_type=F32)
    mixed = mixed + jnp.dot(ob_ref[0], wb_ref[...], preferred_element_type=F32)
    gate = mod_ref[0][gate_row:gate_row + 1]
    o_ref[0] = x_ref[0] + gate * mixed


def _out_odd_kernel(oc_ref, up_ref, u_ref, un_ref, pw_ref, psc_ref, wa_ref, wb_ref, x_ref, mod_ref, o_ref, scr,
                    *, gate_row, seq_len):
    oc = jnp.concatenate([oc_ref[0, j] for j in range(oc_ref.shape[1])], axis=1)
    od = _pool_tile(up_ref, u_ref, un_ref, pw_ref, psc_ref, scr, seq_len)
    mixed = jnp.dot(oc.astype(BF16), wa_ref[...], preferred_element_type=F32)
    mixed = mixed + jnp.dot(od.astype(BF16), wb_ref[...], preferred_element_type=F32)
    gate = mod_ref[0][gate_row:gate_row + 1]
    o_ref[0] = x_ref[0] + gate * mixed


def _out_proj_odd(oc, u, pool_w, pool_scale, w_all, w_layer, x, mod_all, layer, *, gate_row, tm=1024):
    bsz, t, d = x.shape
    half = w_all.shape[1] // 2
    ch = u.shape[2]
    hb = tm // HALO_ROWS
    last = t // HALO_ROWS - 1
    tok = lambda width: pl.BlockSpec((1, tm, width), lambda b, i: (b, i, 0))
    return pl.pallas_call(
        functools.partial(_out_odd_kernel, gate_row=gate_row, seq_len=t),
        grid=(bsz, t // tm),
        in_specs=[
            pl.BlockSpec((1, half // LANES, tm, LANES), lambda b, i: (b, 0, i, 0)),
            pl.BlockSpec((1, HALO_ROWS, ch), lambda b, i: (b, jnp.maximum(i * hb - 1, 0), 0)),
            tok(ch),
            pl.BlockSpec((1, HALO_ROWS, ch), lambda b, i: (b, jnp.minimum((i + 1) * hb, last), 0)),
            pl.BlockSpec(pool_w.shape, lambda b, i: (0, 0, 0)),
            pl.BlockSpec((1, ch), lambda b, i: (0, 0)),
            pl.BlockSpec((None, half, d), lambda b, i: (w_layer, 0, 0)),
            pl.BlockSpec((None, half, d), lambda b, i: (w_layer, 1, 0)),
            tok(d),
            pl.BlockSpec((None, 1, 6, d), lambda b, i: (layer, b, 0, 0)),
        ],
        out_specs=tok(d),
        out_shape=jax.ShapeDtypeStruct((bsz, t, d), F32),
        scratch_shapes=[pltpu.VMEM((tm + 2 * HALO_ROWS, ch), F32)],
        compiler_params=_params(("arbitrary", "arbitrary")),
        name="out_proj_odd",
    )(oc, u, u, u, pool_w, pool_scale.reshape(1, ch), w_all, w_all, x, mod_all)


def _out_proj(parts, w_all, w_layer, x, mod_all, layer, *, gate_row, tm=1024):
    bsz, t, d = x.shape
    half = w_all.shape[1] // 2
    tok = lambda width: pl.BlockSpec((1, tm, width), lambda b, i: (b, i, 0))
    part_specs = [tok(half) for _ in parts]
    return pl.pallas_call(
        functools.partial(_out_even_kernel, gate_row=gate_row),
        grid=(bsz, t // tm),
        in_specs=part_specs + [
            pl.BlockSpec((None, half, d), lambda b, i: (w_layer, 0, 0)),
            pl.BlockSpec((None, half, d), lambda b, i: (w_layer, 1, 0)),
            tok(d),
            pl.BlockSpec((None, 1, 6, d), lambda b, i: (layer, b, 0, 0)),
        ],
        out_specs=tok(d),
        out_shape=jax.ShapeDtypeStruct((bsz, t, d), F32),
        compiler_params=_params(("arbitrary", "arbitrary")),
        name="out_proj_even",
    )(*parts, w_all, w_all, x, mod_all)


def _ffn_up_kernel(xp_ref, x_ref, xn_ref, g_ref, mod_ref, w_ref, cw_ref, cb_ref, o_ref, h_scr, *, shift_row):
    i = pl.program_id(1)
    tm = x_ref.shape[1]
    dff = o_ref.shape[2]
    rows = tm + 2 * HALO_ROWS
    m = mod_ref[0]
    shift, scale = m[shift_row:shift_row + 1], m[shift_row + 1:shift_row + 2]
    g = g_ref[...]
    h_scr[0:HALO_ROWS] = _norm_mod(xp_ref[0], g, shift, scale).astype(BF16)
    h_scr[HALO_ROWS:HALO_ROWS + tm] = _norm_mod(x_ref[0], g, shift, scale).astype(BF16)
    h_scr[HALO_ROWS + tm:] = _norm_mod(xn_ref[0], g, shift, scale).astype(BF16)
    top = jnp.where(i > 0, 1.0, 0.0)
    bottom = jnp.where(i < pl.num_programs(1) - 1, 1.0, 0.0)
    h = h_scr[...]
    for c0 in range(0, dff, FFN_COLS):
        gate = jnp.dot(h, w_ref[:, c0:c0 + FFN_COLS], preferred_element_type=F32)
        gate = jnp.concatenate([gate[:HALO_ROWS] * top, gate[HALO_ROWS:HALO_ROWS + tm],
                                gate[HALO_ROWS + tm:] * bottom], axis=0)
        val = jnp.dot(h, w_ref[:, dff + c0:dff + c0 + FFN_COLS], preferred_element_type=F32)
        cw = 0.5 * cw_ref[:, c0:c0 + FFN_COLS]
        half = 0.5 * cb_ref[:, c0:c0 + FFN_COLS] + gate * cw[1:2]
        half = half + pltpu.roll(gate, 1, 0) * cw[0:1]
        half = half + pltpu.roll(gate, rows - 1, 0) * cw[2:3]
        act = half * (1.0 + jnp.tanh(half)) * val
        o_ref[0, :, c0:c0 + FFN_COLS] = act[HALO_ROWS:HALO_ROWS + tm].astype(o_ref.dtype)


def _ffn_up(x, g_all, mod_all, w_all, conv_w_all, conv_b_all, layer, *, shift_row, tm=1024):
    bsz, t, d = x.shape
    dff = w_all.shape[2] // 2
    hb = tm // HALO_ROWS
    last = t // HALO_ROWS - 1
    return pl.pallas_call(
        functools.partial(_ffn_up_kernel, shift_row=shift_row),
        grid=(bsz, t // tm),
        in_specs=[
            pl.BlockSpec((1, HALO_ROWS, d), lambda b, i: (b, jnp.maximum(i * hb - 1, 0), 0)),
            pl.BlockSpec((1, tm, d), lambda b, i: (b, i, 0)),
            pl.BlockSpec((1, HALO_ROWS, d), lambda b, i: (b, jnp.minimum((i + 1) * hb, last), 0)),
            pl.BlockSpec((None, 1, d), lambda b, i: (layer, 0, 0)),
            pl.BlockSpec((None, 1, 6, d), lambda b, i: (layer, b, 0, 0)),
            pl.BlockSpec((None, d, 2 * dff), lambda b, i: (layer, 0, 0), pipeline_mode=pl.Buffered(1)),
            pl.BlockSpec((None, FFN_CONV, dff), lambda b, i: (layer, 0, 0)),
            pl.BlockSpec((None, 1, dff), lambda b, i: (layer, 0, 0)),
        ],
        out_specs=pl.BlockSpec((1, tm, dff), lambda b, i: (b, i, 0)),
        out_shape=jax.ShapeDtypeStruct((bsz, t, dff), BF16),
        scratch_shapes=[pltpu.VMEM((tm + 2 * HALO_ROWS, d), BF16)],
        compiler_params=_params(("arbitrary", "arbitrary")),
        name="ffn_up_conv",
    )(x, x, x, g_all, mod_all, w_all, conv_w_all, conv_b_all)


def _ffn_down_kernel(a_ref, w_ref, x_ref, mod_ref, *rest, gate_row):
    y = jnp.dot(a_ref[0], w_ref[...], preferred_element_type=F32)
    x = x_ref[0] + mod_ref[0][gate_row:gate_row + 1] * y
    if len(rest) == 2:
        g_ref, o_ref = rest
        x = x * lax.rsqrt(jnp.mean(x * x, axis=-1, keepdims=True) + RMS_EPS) * g_ref[...]
    else:
        o_ref, = rest
    o_ref[0] = x


def _ffn_down(a, w_all, x, mod_all, layer, final_g, *, gate_row, tm=512):
    bsz, t, d = x.shape
    dff = a.shape[-1]
    extra = [] if final_g is None else [final_g]
    return pl.pallas_call(
        functools.partial(_ffn_down_kernel, gate_row=gate_row),
        grid=(bsz, t // tm),
        in_specs=[
            pl.BlockSpec((1, tm, dff), lambda b, i: (b, i, 0)),
            pl.BlockSpec((None, dff, d), lambda b, i: (layer, 0, 0)),
            pl.BlockSpec((1, tm, d), lambda b, i: (b, i, 0)),
            pl.BlockSpec((None, 1, 6, d), lambda b, i: (layer, b, 0, 0)),
        ] + [pl.BlockSpec((1, d), lambda b, i: (0, 0)) for _ in extra],
        out_specs=pl.BlockSpec((1, tm, d), lambda b, i: (b, i, 0)),
        out_shape=jax.ShapeDtypeStruct((bsz, t, d), F32),
        compiler_params=_params(("arbitrary", "arbitrary")),
        name="ffn_down",
    )(a, w_all, x, mod_all, *extra)


def kernel(x, c, ada_w, ada_b, norm_mix_g, norm_ffn_g, even_w_in, even_w_out, hgrn_lb_logits, hgrn_norm_g, na_rpb, odd_w_in, odd_w_out, pool_w, pool_scale, ffn_w_up, ffn_conv_w, ffn_conv_b, ffn_w_down, final_norm_g):
    bsz, t, d = x.shape
    depth = ada_w.shape[0]
    a_width = hgrn_lb_logits.shape[1]
    n_groups = len(C_CONFIGS)
    c_width = C_HPG * ATT_DH
    n_c = 3 * n_groups * c_width

    lb_soft = jax.nn.softmax(hgrn_lb_logits.astype(F32), axis=0)
    lower_bounds = jnp.cumsum(lb_soft, axis=0) - lb_soft[0]
    slopes = jnp.exp2(-8.0 * jnp.arange(1, n_groups * C_HPG + 1, dtype=F32) / (n_groups * C_HPG))
    slopes = slopes.reshape(n_groups, C_HPG)

    mod_all = _ada_mod(c, ada_w, ada_b).reshape(depth, bsz, 6, d)

    g_mix, g_ffn = norm_mix_g.reshape(depth, 1, d), norm_ffn_g.reshape(depth, 1, d)
    even_in, odd_in = even_w_in.astype(BF16), odd_w_in.astype(BF16)
    even_out, odd_out = even_w_out.astype(BF16), odd_w_out.astype(BF16)
    w_up, w_down = ffn_w_up.astype(BF16), ffn_w_down.astype(BF16)
    conv_b = ffn_conv_b.reshape(depth, 1, -1)
    pool_w_bf = pool_w.astype(BF16)
    na_tables = _na_bias_table(na_rpb, t // GRID_W)
    qkv = lambda g: tuple(((k * n_groups + g) * c_width, c_width) for k in range(3))

    for l in range(depth):
        if l % 2 == 0:
            e = l // 2
            b_width = (even_in.shape[2] - 5 * a_width) // 3
            b_cols = tuple((5 * a_width + k * b_width, b_width) for k in range(3))
            z_a, z_b = _in_proj(x, g_mix, mod_all, even_in, l, e, (((0, 5 * a_width),), b_cols),
                                (F32, BF16), tm=512)
            o_a = _hgrn2(z_a, lower_bounds[e], hgrn_norm_g[e])
            o_b = _neighbourhood_attention(z_b, na_tables, e)
            x = _out_proj([o_a, o_b], even_out, e, x, mod_all, l, gate_row=2)
        else:
            o_i = l // 2
            acc = None
            for g, (_, dil) in enumerate(C_CONFIGS):
                if dil == 1:
                    z_g, z_d = _in_proj(x, g_mix, mod_all, odd_in, l, o_i, (qkv(g), ((n_c, odd_in.shape[2] - n_c),)),
                                        (BF16, F32), tm=DIL_TILE[dil])
                else:
                    z_g, = _in_proj(x, g_mix, mod_all, odd_in, l, o_i, (qkv(g),), (BF16,),
                                    tm=DIL_TILE[dil], perm=dil)
                acc = _dilated_group(z_g, dil, slopes[g], DIL_TILE[dil], acc, emit_lse=g < n_groups - 1)
            x = _out_proj_odd(acc[0], z_d, pool_w_bf[o_i], pool_scale[o_i], odd_out, o_i, x, mod_all, l, gate_row=2)
        a = _ffn_up(x, g_ffn, mod_all, w_up, ffn_conv_w, conv_b, l, shift_row=3)
        final_g = final_norm_g.reshape(1, d) if l == depth - 1 else None
        x = _ffn_down(a, w_down, x, mod_all, l, final_g, gate_row=5)
    return x
```
